```python
import math
import jax, jax.numpy as jnp
from jax import lax
import numpy as np

D_MODEL = 2048
BATCH = 8
SEQ = 2048
DEPTH = 4

HEAD_DIM = 128
MIXER_HEADS = D_MODEL // 256
BRANCH_WIDTH = MIXER_HEADS * HEAD_DIM
N_BRANCHES = 3
FOX_HEADS = MIXER_HEADS
FOX_HEAD_DIM = HEAD_DIM
FOX_W = FOX_HEADS * FOX_HEAD_DIM
Q_BLOCK = 128
MLA_HEADS = MIXER_HEADS
MLA_Q_RANK = D_MODEL // 4
MLA_KV_RANK = D_MODEL // 4
MLA_NOPE_DIM = 128
MLA_ROPE_DIM = 64
MLA_V_DIM = 128
ROPE_THETA = 10000.0
GDN_HEADS = MIXER_HEADS
GDN_HEAD_DIM = HEAD_DIM
GDN_W = GDN_HEADS * GDN_HEAD_DIM
GDN_CONV = 4
GDN_CHUNK = 64
D_FF = 4 * D_MODEL
EPS = 1e-6

IN_SPLIT_SIZES = (FOX_W, FOX_W, FOX_W, FOX_HEADS,
                  MLA_Q_RANK, MLA_KV_RANK, MLA_ROPE_DIM,
                  3 * GDN_W, GDN_W, GDN_HEADS, GDN_HEADS,
                  N_BRANCHES * D_MODEL)
D_IN = sum(IN_SPLIT_SIZES)

kernel_name = 'hybrid_fox_mla_gdn_gated_parallel_block'


def rms_norm(x, gain):
    xf = x.astype(jnp.float32)
    y = xf * lax.rsqrt(jnp.mean(xf * xf, axis=-1, keepdims=True) + EPS)
    return (y * gain.astype(jnp.float32)).astype(x.dtype)


def l2_norm(x):
    return x * lax.rsqrt(jnp.sum(x * x, axis=-1, keepdims=True) + EPS)


def rope_tables(seq):
    inv_freq = ROPE_THETA ** (-jnp.arange(0, MLA_ROPE_DIM, 2, dtype=jnp.float32) / MLA_ROPE_DIM)
    ang = jnp.arange(seq, dtype=jnp.float32)[:, None] * inv_freq[None, :]
    return jnp.cos(ang), jnp.sin(ang)


def apply_rope(x, cos, sin):
    xf = x.astype(jnp.float32)
    x1, x2 = jnp.split(xf, 2, axis=-1)
    return jnp.concatenate([x1 * cos - x2 * sin, x2 * cos + x1 * sin], axis=-1).astype(x.dtype)


def causal_block_attention(q, k, v, cum_log_f=None):
    B, H, S, Dk = q.shape
    Dv = v.shape[-1]
    nb = S // Q_BLOCK
    scale = Dk ** -0.5
    xs = [jnp.arange(nb), q.reshape(B, H, nb, Q_BLOCK, Dk).transpose(2, 0, 1, 3, 4)]
    if cum_log_f is not None:
        xs.append(cum_log_f.reshape(B, H, nb, Q_BLOCK).transpose(2, 0, 1, 3))
    kpos = jnp.arange(S)

    def one_block(args):
        i, q_blk = args[0], args[1]
        s = jnp.einsum('bhqd,bhkd->bhqk', q_blk, k).astype(jnp.float32) * scale
        if cum_log_f is not None:
            s = s + args[2][..., :, None] - cum_log_f[:, :, None, :]
        qpos = i * Q_BLOCK + jnp.arange(Q_BLOCK)
        s = jnp.where(kpos[None, :] <= qpos[:, None], s, -jnp.inf)
        p = jax.nn.softmax(s, axis=-1)
        return jnp.einsum('bhqk,bhkd->bhqd', p.astype(v.dtype), v)

    out = lax.map(one_block, tuple(xs))
    return out.transpose(1, 2, 0, 3, 4).reshape(B, H, S, Dv)


def fox_mixer(q, k, v, f_logit, f_bias):
    B, S, _ = q.shape
    heads = lambda t: t.reshape(B, S, FOX_HEADS, FOX_HEAD_DIM).transpose(0, 2, 1, 3)
    log_f = jax.nn.log_sigmoid(f_logit.astype(jnp.float32) + f_bias.astype(jnp.float32))
    cum = jnp.cumsum(log_f, axis=1).transpose(0, 2, 1)
    o = causal_block_attention(heads(q), heads(k), heads(v), cum)
    return o.transpose(0, 2, 1, 3).reshape(B, S, FOX_W)


def mla_mixer(c_q, c_kv, k_pe, q_norm, kv_norm, w_uq, w_ukv, cos, sin):
    B, S, _ = c_q.shape
    q = jnp.einsum('bsr,rhd->bshd', rms_norm(c_q, q_norm), w_uq)
    q_nope, q_pe = q[..., :MLA_NOPE_DIM], q[..., MLA_NOPE_DIM:]
    q_pe = apply_rope(q_pe, cos[:, None, :], sin[:, None, :])
    kv = jnp.einsum('bsr,rhd->bshd', rms_norm(c_kv, kv_norm), w_ukv)
    k_nope, v = kv[..., :MLA_NOPE_DIM], kv[..., MLA_NOPE_DIM:]
    k_pe = apply_rope(k_pe, cos, sin)
    q = jnp.concatenate([q_nope, q_pe], axis=-1)
    k = jnp.concatenate([k_nope, jnp.broadcast_to(k_pe[:, :, None, :], (B, S, MLA_HEADS, MLA_ROPE_DIM))], axis=-1)
    t = lambda a: a.transpose(0, 2, 1, 3)
    o = causal_block_attention(t(q), t(k), t(v))
    return o.transpose(0, 2, 1, 3).reshape(B, S, MLA_HEADS * MLA_V_DIM)


def causal_depthwise_conv(x, w):
    K, C = w.shape
    return lax.conv_general_dilated(x, w[:, None, :].astype(x.dtype), window_strides=(1,),
                                    padding=[(K - 1, 0)], dimension_numbers=('NWC', 'WIO', 'NWC'),
                                    feature_group_count=C)


def gated_delta_rule_chunked(q, k, v, g, beta):
    B, H, S, Dk = q.shape
    Dv = v.shape[-1]
    C = GDN_CHUNK
    N = S // C
    chunk = lambda t: t.reshape(B, H, N, C, *t.shape[3:])
    q = chunk(q * Dk ** -0.5)
    k = chunk(k)
    v = chunk(v)
    beta = chunk(beta)
    g = jnp.cumsum(chunk(g), axis=-1)
    incl = jnp.tril(jnp.ones((C, C), dtype=bool))
    strict = jnp.tril(jnp.ones((C, C), dtype=bool), -1)
    decay = jnp.exp(jnp.where(incl, g[..., :, None] - g[..., None, :], -jnp.inf))
    k_beta = k * beta[..., None]
    lower = jnp.where(strict, jnp.einsum('bhnid,bhnjd->bhnij', k_beta, k) * decay, 0.0)
    eye = jnp.eye(C, dtype=q.dtype)
    t_inv = lax.linalg.triangular_solve(lower + eye, jnp.broadcast_to(eye, lower.shape),
                                        left_side=True, lower=True)
    u = jnp.einsum('bhnij,bhnjd->bhnid', t_inv, v * beta[..., None])
    w = jnp.einsum('bhnij,bhnjd->bhnid', t_inv, k_beta * jnp.exp(g)[..., None])
    intra = jnp.where(incl, jnp.einsum('bhnid,bhnjd->bhnij', q, k) * decay, 0.0)
    q_dec = q * jnp.exp(g)[..., None]
    k_dec = k * jnp.exp(g[..., -1:] - g)[..., None]
    chunk_decay = jnp.exp(g[..., -1])

    def step(state, xs):
        u_c, w_c, a_c, qd_c, kd_c, cd_c = xs
        v_new = u_c - jnp.einsum('bhik,bhkv->bhiv', w_c, state)
        out = jnp.einsum('bhik,bhkv->bhiv', qd_c, state) + jnp.einsum('bhij,bhjv->bhiv', a_c, v_new)
        state = state * cd_c[..., None, None] + jnp.einsum('bhik,bhiv->bhkv', kd_c, v_new)
        return state, out

    to_front = lambda t: jnp.moveaxis(t, 2, 0)
    state0 = jnp.zeros((B, H, Dk, Dv), q.dtype)
    _, out = lax.scan(step, state0, (to_front(u), to_front(w), to_front(intra),
                                     to_front(q_dec), to_front(k_dec), to_front(chunk_decay)))
    return jnp.moveaxis(out, 0, 2).reshape(B, H, S, Dv)


def gdn_mixer(qkv, z, b_logit, a_logit, conv_w, a_log, dt_bias, out_norm):
    B, S, _ = qkv.shape
    qkv_c = jax.nn.silu(causal_depthwise_conv(qkv, conv_w))
    q, k, v = jnp.split(qkv_c, 3, axis=-1)
    heads = lambda t: t.reshape(B, S, GDN_HEADS, GDN_HEAD_DIM).transpose(0, 2, 1, 3).astype(jnp.float32)
    q = l2_norm(heads(q))
    k = l2_norm(heads(k))
    v = heads(v)
    beta = jax.nn.sigmoid(b_logit.astype(jnp.float32)).transpose(0, 2, 1)
    g = (-jnp.exp(a_log.astype(jnp.float32))
         * jax.nn.softplus(a_logit.astype(jnp.float32) + dt_bias.astype(jnp.float32))).transpose(0, 2, 1)
    o = gated_delta_rule_chunked(q, k, v, g, beta).transpose(0, 2, 1, 3)
    zh = z.reshape(B, S, GDN_HEADS, GDN_HEAD_DIM).astype(jnp.float32)
    o = rms_norm(o, out_norm) * jax.nn.silu(zh)
    return o.reshape(B, S, GDN_W).astype(qkv.dtype)


def setup_inputs(seed: int = 0) -> dict:
    key = jax.random.key(seed)
    ks = jax.random.split(key, 20)
    nrm = lambda k, shape, fan_in: jax.random.normal(k, shape, jnp.float32) * (fan_in ** -0.5)
    gain = lambda k, shape: 1.0 + 0.02 * jax.random.normal(k, shape, jnp.float32)
    dt = jnp.exp(jax.random.uniform(ks[10], (DEPTH, GDN_HEADS), jnp.float32,
                                    minval=math.log(1e-3), maxval=math.log(1e-1)))
    return {
        'x': jax.random.normal(ks[0], (BATCH, SEQ, D_MODEL), jnp.float32),
        'attn_norm': gain(ks[1], (DEPTH, D_MODEL)),
        'w_in': nrm(ks[2], (DEPTH, D_MODEL, D_IN), D_MODEL),
        'fox_fgate_bias': 2.0 + 0.5 * jax.random.normal(ks[3], (DEPTH, FOX_HEADS), jnp.float32),
        'mla_q_norm': gain(ks[4], (DEPTH, MLA_Q_RANK)),
        'mla_kv_norm': gain(ks[5], (DEPTH, MLA_KV_RANK)),
        'w_mla_uq': nrm(ks[6], (DEPTH, MLA_Q_RANK, MLA_HEADS, MLA_NOPE_DIM + MLA_ROPE_DIM), MLA_Q_RANK),
        'w_mla_ukv': nrm(ks[7], (DEPTH, MLA_KV_RANK, MLA_HEADS, MLA_NOPE_DIM + MLA_V_DIM), MLA_KV_RANK),
        'gdn_conv': nrm(ks[8], (DEPTH, GDN_CONV, 3 * GDN_W), GDN_CONV),
        'gdn_a_log': jnp.log(jax.random.uniform(ks[9], (DEPTH, GDN_HEADS), jnp.float32, minval=1.0, maxval=16.0)),
        'gdn_dt_bias': dt + jnp.log(-jnp.expm1(-dt)),
        'gdn_out_norm': gain(ks[11], (DEPTH, GDN_HEAD_DIM)),
        'w_branch': nrm(ks[12], (DEPTH, N_BRANCHES, BRANCH_WIDTH, D_MODEL), BRANCH_WIDTH),
        'w_out': nrm(ks[13], (DEPTH, D_MODEL, D_MODEL), D_MODEL),
        'mlp_norm': gain(ks[14], (DEPTH, D_MODEL)),
        'w_up': nrm(ks[15], (DEPTH, D_MODEL, D_FF), D_MODEL),
        'w_down': nrm(ks[16], (DEPTH, D_FF, D_MODEL), D_FF),
        'final_norm': gain(ks[17], (D_MODEL,)),
    }


def reference(x, attn_norm, w_in, fox_fgate_bias, mla_q_norm, mla_kv_norm, w_mla_uq, w_mla_ukv,
              gdn_conv, gdn_a_log, gdn_dt_bias, gdn_out_norm, w_branch, w_out, mlp_norm,
              w_up, w_down, final_norm):
    B, S, _ = x.shape
    cos, sin = rope_tables(S)
    split_idx = np.cumsum(IN_SPLIT_SIZES)[:-1].tolist()
    for l in range(DEPTH):
        u = rms_norm(x, attn_norm[l])
        proj = jnp.einsum('bsd,de->bse', u, w_in[l])
        (fq, fk, fv, ff, cq, ckv, kpe, gqkv, gz, gb, ga, gate) = jnp.split(proj, split_idx, axis=-1)
        o_fox = fox_mixer(fq, fk, fv, ff, fox_fgate_bias[l])
        o_mla = mla_mixer(cq, ckv, kpe, mla_q_norm[l], mla_kv_norm[l], w_mla_uq[l], w_mla_ukv[l], cos, sin)
        o_gdn = gdn_mixer(gqkv, gz, gb, ga, gdn_conv[l], gdn_a_log[l], gdn_dt_bias[l], gdn_out_norm[l])
        gates = jax.nn.sigmoid(gate.astype(jnp.float32)).astype(x.dtype).reshape(B, S, N_BRANCHES, D_MODEL)
        merged = (gates[:, :, 0] * jnp.einsum('bsc,cd->bsd', o_fox, w_branch[l, 0])
                  + gates[:, :, 1] * jnp.einsum('bsc,cd->bsd', o_mla, w_branch[l, 1])
                  + gates[:, :, 2] * jnp.einsum('bsc,cd->bsd', o_gdn, w_branch[l, 2]))
        x = x + jnp.einsum('bsd,de->bse', merged, w_out[l])
        h = rms_norm(x, mlp_norm[l])
        hidden = jnp.square(jax.nn.relu(jnp.einsum('bsd,df->bsf', h, w_up[l])))
        x = x + jnp.einsum('bsf,fd->bsd', hidden, w_down[l])
    return rms_norm(x, final_norm)
```

```python
import functools
import math

import numpy as np
import jax
import jax.numpy as jnp
from jax import lax
from jax.experimental import pallas as pl
from jax.experimental.pallas import tpu as pltpu

F32 = jnp.float32
BF16 = jnp.bfloat16
HIGHEST = lax.Precision.HIGHEST

D_MODEL = 2048
HEAD_DIM = 128
HEADS = 8
BRANCH_W = HEADS * HEAD_DIM
MLA_RANK = 512
MLA_ROPE = 64
MLA_QK = 256
ROPE_THETA = 10000.0
GDN_CONV = 4
D_FF = 4 * D_MODEL
EPS = 1e-6

LANES = 128
VMEM_LIMIT = 52 * 1024 * 1024

C_FQ, C_FK, C_FV = 0, 1024, 2048
C_CQ, C_CKV = 3072, 3584
C_GQ, C_GK, C_GV, C_GZ = 4096, 5120, 6144, 7168
C_GATE = 8192
N_MAIN = C_GATE + 3 * D_MODEL
N_SMALL = 3 * LANES
S_FF, S_GB, S_GA = 0, 8, 16

GDN_C = 128
INV_PRECISION = HIGHEST


def _cparams(sem):
    return pltpu.CompilerParams(dimension_semantics=sem, vmem_limit_bytes=VMEM_LIMIT)


def _sigmoid(x):
    return 1.0 / (1.0 + jnp.exp(-x))


def _softplus(x):
    return jnp.maximum(x, 0.0) + jnp.log(1.0 + jnp.exp(-jnp.abs(x)))


def _rms(x, gain):
    return x * lax.rsqrt(jnp.mean(x * x, axis=-1, keepdims=True) + EPS) * gain


def _proj_kernel(x_ref, g_ref, w_ref, ws_ref, o_ref, os_ref, u_ref):
    @pl.when(pl.program_id(1) == 0)
    def _():
        u = _rms(x_ref[...], g_ref[...]).astype(BF16)
        u_ref[...] = u
        os_ref[...] = jnp.dot(u, ws_ref[...], preferred_element_type=F32)

    o_ref[...] = jnp.dot(u_ref[...], w_ref[...], preferred_element_type=F32).astype(BF16)


def _proj_in(x2, gain, w_main, w_small, tm=1024, tn=512):
    T = x2.shape[0]
    return pl.pallas_call(
        _proj_kernel,
        grid=(T // tm, N_MAIN // tn),
        in_specs=[
            pl.BlockSpec((tm, D_MODEL), lambda i, j: (i, 0)),
            pl.BlockSpec((1, D_MODEL), lambda i, j: (0, 0)),
            pl.BlockSpec((D_MODEL, tn), lambda i, j: (0, j)),
            pl.BlockSpec((D_MODEL, N_SMALL), lambda i, j: (0, 0)),
        ],
        out_specs=[
            pl.BlockSpec((tm, tn), lambda i, j: (i, j)),
            pl.BlockSpec((tm, N_SMALL), lambda i, j: (i, 0)),
        ],
        out_shape=[
            jax.ShapeDtypeStruct((T, N_MAIN), BF16),
            jax.ShapeDtypeStruct((T, N_SMALL), F32),
        ],
        scratch_shapes=[pltpu.VMEM((tm, D_MODEL), BF16)],
        compiler_params=_cparams(("parallel", "arbitrary")),
        name="proj_in",
    )(x2, gain, w_main, w_small)


def _cumsum_kernel(s_ref, b_ref, c_ref, ct_ref):
    S = s_ref.shape[1]
    x = s_ref[0] + b_ref[...]
    logf = -_softplus(-x)
    r = lax.broadcasted_iota(jnp.int32, (LANES, LANES), 0)
    c = lax.broadcasted_iota(jnp.int32, (LANES, LANES), 1)
    tri = (r >= c).astype(F32)
    carry = jnp.zeros((1, LANES), F32)
    for blk in range(S // LANES):
        seg = logf[blk * LANES:(blk + 1) * LANES]
        cs = jnp.dot(tri, seg, precision=HIGHEST, preferred_element_type=F32) + carry
        c_ref[0, blk * LANES:(blk + 1) * LANES, :] = cs
        carry = cs[LANES - 1:LANES, :]
    ct_ref[0] = c_ref[0].T[:HEADS]


def _fox_cumsum(small3, bias_row):
    B, S, _ = small3.shape
    return pl.pallas_call(
        _cumsum_kernel,
        grid=(B,),
        in_specs=[
            pl.BlockSpec((1, S, LANES), lambda b: (b, 0, 2)),
            pl.BlockSpec((1, LANES), lambda b: (0, 0)),
        ],
        out_specs=[
            pl.BlockSpec((1, S, LANES), lambda b: (b, 0, 0)),
            pl.BlockSpec((1, HEADS, S), lambda b: (b, 0, 0)),
        ],
        out_shape=[
            jax.ShapeDtypeStruct((B, S, LANES), F32),
            jax.ShapeDtypeStruct((B, HEADS, S), F32),
        ],
        compiler_params=_cparams(("parallel",)),
        name="fox_cumsum",
    )(small3, bias_row)


def _attn_kernel(*refs, dk, scale, decay, tq):
    if decay:
        q_ref, k_ref, v_ref, cq_ref, ck_ref, o_ref = refs
    else:
        q_ref, k_ref, v_ref, o_ref = refs
    i = pl.program_id(1)
    row = lax.broadcasted_iota(jnp.int32, (tq, tq), 0)
    col = lax.broadcasted_iota(jnp.int32, (tq, tq), 1)
    causal = col <= row
    for h in range(HEADS):
        q = q_ref[0, :, h * dk:(h + 1) * dk]
        if decay:
            cq = cq_ref[0, :, h:h + 1]

        def step(j, carry, masked, h=h, q=q, cq=cq if decay else None):
            m, l, acc = carry
            off = pl.multiple_of(j * tq, tq)
            k = k_ref[0, pl.ds(off, tq), h * dk:(h + 1) * dk]
            v = v_ref[0, pl.ds(off, tq), h * HEAD_DIM:(h + 1) * HEAD_DIM]
            s = lax.dot_general(q, k, (((1,), (1,)), ((), ())),
                                preferred_element_type=F32) * scale
            if decay:
                s = s + cq - ck_ref[0, h:h + 1, pl.ds(off, tq)]
            if masked:
                s = jnp.where(causal, s, -jnp.inf)
            m_new = jnp.maximum(m, jnp.max(s, axis=-1, keepdims=True))
            p = jnp.exp(s - m_new)
            alpha = jnp.exp(m - m_new)
            l = alpha * l + jnp.sum(p, axis=-1, keepdims=True)
            acc = alpha * acc + jnp.dot(p.astype(BF16), v, preferred_element_type=F32)
            return m_new, l, acc

        init = (jnp.full((tq, 1), -jnp.inf, F32), jnp.zeros((tq, 1), F32),
                jnp.zeros((tq, HEAD_DIM), F32))
        carry = lax.fori_loop(0, i, functools.partial(step, masked=False), init)
        _, l, acc = step(i, carry, masked=True)
        o_ref[0, :, h * HEAD_DIM:(h + 1) * HEAD_DIM] = (acc / l).astype(BF16)


def _attention(q_arr, k_arr, v_arr, q_blk, k_blk, v_blk, dk, scale, cum=None, tq=256):
    B, S, _ = q_arr.shape
    decay = cum is not None
    in_specs = [
        pl.BlockSpec((1, tq, HEADS * dk), lambda b, i: (b, i, q_blk)),
        pl.BlockSpec((1, S, HEADS * dk), lambda b, i: (b, 0, k_blk)),
        pl.BlockSpec((1, S, BRANCH_W), lambda b, i: (b, 0, v_blk)),
    ]
    args = [q_arr, k_arr, v_arr]
    if decay:
        in_specs += [
            pl.BlockSpec((1, tq, LANES), lambda b, i: (b, i, 0)),
            pl.BlockSpec((1, HEADS, S), lambda b, i: (b, 0, 0)),
        ]
        args += list(cum)
    return pl.pallas_call(
        functools.partial(_attn_kernel, dk=dk, scale=scale, decay=decay, tq=tq),
        grid=(B, S // tq),
        in_specs=in_specs,
        out_specs=pl.BlockSpec((1, tq, BRANCH_W), lambda b, i: (b, i, 0)),
        out_shape=jax.ShapeDtypeStruct((B, S, BRANCH_W), BF16),
        compiler_params=_cparams(("parallel", "arbitrary")),
        name="fox_attn" if decay else "mla_attn",
    )(*args)


def _mla_prep_kernel(cq_ref, ckv_ref, sm_ref, qn_ref, kvn_ref, wqa_ref, wqb_ref, wk_ref, wv_ref,
                     cos_ref, sin_ref, q_ref, k_ref, v_ref):
    cos = cos_ref[...]
    sin = sin_ref[...]
    qn = _rms(cq_ref[...].astype(F32), qn_ref[...]).astype(BF16)
    a = jnp.dot(qn, wqa_ref[...], preferred_element_type=F32)
    bsw = jnp.dot(qn, wqb_ref[...], preferred_element_type=F32)
    kvn = _rms(ckv_ref[...].astype(F32), kvn_ref[...]).astype(BF16)
    kn = jnp.dot(kvn, wk_ref[...], preferred_element_type=F32)
    v_ref[...] = jnp.dot(kvn, wv_ref[...], preferred_element_type=F32).astype(BF16)
    sm = sm_ref[...]
    kpe = (sm[:, 0:LANES] * cos + sm[:, LANES:2 * LANES] * sin).astype(BF16)
    for h in range(HEADS):
        lo = h * MLA_QK
        q_ref[:, lo:lo + LANES] = a[:, lo:lo + LANES].astype(BF16)
        q_ref[:, lo + LANES:lo + 2 * LANES] = (
            a[:, lo + LANES:lo + 2 * LANES] * cos + bsw[:, h * LANES:(h + 1) * LANES] * sin
        ).astype(BF16)
        k_ref[:, lo:lo + LANES] = kn[:, h * LANES:(h + 1) * LANES].astype(BF16)
        k_ref[:, lo + LANES:lo + 2 * LANES] = kpe


def _mla_prep(p_main, p_small, qn, kvn, wqa, wqb, wk, wv, cos_t, sin_t, S, tm=512):
    T = p_main.shape[0]
    nS = S // tm
    const = lambda i: (0, 0)
    return pl.pallas_call(
        _mla_prep_kernel,
        grid=(T // tm,),
        in_specs=[
            pl.BlockSpec((tm, MLA_RANK), lambda i: (i, C_CQ // MLA_RANK)),
            pl.BlockSpec((tm, MLA_RANK), lambda i: (i, C_CKV // MLA_RANK)),
            pl.BlockSpec((tm, N_SMALL), lambda i: (i, 0)),
            pl.BlockSpec((1, MLA_RANK), const),
            pl.BlockSpec((1, MLA_RANK), const),
            pl.BlockSpec((MLA_RANK, HEADS * MLA_QK), const),
            pl.BlockSpec((MLA_RANK, HEADS * LANES), const),
            pl.BlockSpec((MLA_RANK, BRANCH_W), const),
            pl.BlockSpec((MLA_RANK, BRANCH_W), const),
            pl.BlockSpec((tm, LANES), lambda i: (i % nS, 0)),
            pl.BlockSpec((tm, LANES), lambda i: (i % nS, 0)),
        ],
        out_specs=[
            pl.BlockSpec((tm, HEADS * MLA_QK), lambda i: (i, 0)),
            pl.BlockSpec((tm, HEADS * MLA_QK), lambda i: (i, 0)),
            pl.BlockSpec((tm, BRANCH_W), lambda i: (i, 0)),
        ],
        out_shape=[
            jax.ShapeDtypeStruct((T, HEADS * MLA_QK), BF16),
            jax.ShapeDtypeStruct((T, HEADS * MLA_QK), BF16),
            jax.ShapeDtypeStruct((T, BRANCH_W), BF16),
        ],
        compiler_params=_cparams(("parallel",)),
        name="mla_prep",
    )(p_main, p_main, p_small, qn, kvn, wqa, wqb, wk, wv, cos_t, sin_t)


def _gdn_kernel(q_ref, k_ref, v_ref, z_ref, sm_ref, cwq_ref, cwk_ref, cwv_ref, hp_ref, on_ref,
                mk_ref, o_ref, qn_s, kn_s, vn_s, g_s, b_s, u_s, w_s, a_s, qd_s, kd_s, cd_s, o_s):
    h = pl.program_id(1)
    S = q_ref.shape[1]
    C = GDN_C
    rows = lax.broadcasted_iota(jnp.int32, (S, LANES), 0)
    lane = lax.broadcasted_iota(jnp.int32, (S, LANES), 1)

    def conv_silu(x_ref, cw_ref):
        x = x_ref[0].astype(F32)
        w = cw_ref[...]
        y = x * w[GDN_CONV - 1:GDN_CONV]
        for sft in range(1, GDN_CONV):
            xs = jnp.where(rows >= sft, pltpu.roll(x, sft, 0), 0.0)
            y = y + xs * w[GDN_CONV - 1 - sft:GDN_CONV - sft]
        return y * _sigmoid(y)

    def l2n(x):
        return x * lax.rsqrt(jnp.sum(x * x, axis=-1, keepdims=True) + EPS)

    qn_s[...] = l2n(conv_silu(q_ref, cwq_ref)) * (HEAD_DIM ** -0.5)
    kn_s[...] = l2n(conv_silu(k_ref, cwk_ref))
    vn_s[...] = conv_silu(v_ref, cwv_ref)

    seg = sm_ref[0, :, 2 * LANES:3 * LANES]
    b_logit = jnp.sum(jnp.where(lane == S_GB + h, seg, 0.0), axis=-1, keepdims=True)
    a_logit = jnp.sum(jnp.where(lane == S_GA + h, seg, 0.0), axis=-1, keepdims=True)
    a_log = hp_ref[0, 0:1, :]
    dt_b = hp_ref[0, 1:2, :]
    b_s[...] = jnp.broadcast_to(_sigmoid(b_logit), (S, LANES))
    g_s[...] = -jnp.exp(a_log) * _softplus(a_logit + dt_b)

    tri = mk_ref[0]
    eye = mk_ref[1]
    incl = tri > 0.5
    strict = tri - eye

    def local(n, _):
        sl = pl.ds(pl.multiple_of(n * C, C), C)
        q = qn_s[sl, :]
        k = kn_s[sl, :]
        v = vn_s[sl, :]
        beta = b_s[sl, :]
        gc = jnp.dot(tri, g_s[sl, :], precision=HIGHEST, preferred_element_type=F32)
        decay = jnp.exp(jnp.where(incl, gc - gc.T, -jnp.inf))
        kb = k * beta
        kbf = k.astype(BF16)
        nt = (((1,), (1,)), ((), ()))
        a = lax.dot_general(kb.astype(BF16), kbf, nt, preferred_element_type=F32) * decay * strict
        a_s[sl, :] = lax.dot_general(q.astype(BF16), kbf, nt, preferred_element_type=F32) * decay
        t = eye - a * mk_ref[2]
        for lvl in range(1, int(math.log2(C))):
            cs = a * mk_ref[2 + lvl]
            tc = jnp.dot(t, cs, precision=INV_PRECISION, preferred_element_type=F32)
            t = t - jnp.dot(tc, t, precision=INV_PRECISION, preferred_element_type=F32)
        eg = jnp.exp(gc)
        tb = t.astype(BF16)
        u_s[sl, :] = jnp.dot(tb, (v * beta).astype(BF16), preferred_element_type=F32)
        w_s[sl, :] = jnp.dot(tb, (kb * eg).astype(BF16), preferred_element_type=F32)
        qd_s[sl, :] = q * eg
        glast = gc[C - 1:C, :]
        kd_s[sl, :] = (k * jnp.exp(glast - gc)).T
        cd_s[n] = jnp.broadcast_to(jnp.exp(glast), (8, LANES))
        return 0

    lax.fori_loop(0, S // C, local, 0)

    def scan(n, state):
        sl = pl.ds(pl.multiple_of(n * C, C), C)
        sb = state.astype(BF16)
        vnew = u_s[sl, :] - jnp.dot(w_s[sl, :].astype(BF16), sb, preferred_element_type=F32)
        vb = vnew.astype(BF16)
        o_s[sl, :] = (jnp.dot(qd_s[sl, :].astype(BF16), sb, preferred_element_type=F32)
                      + jnp.dot(a_s[sl, :].astype(BF16), vb, preferred_element_type=F32))
        cd = cd_s[n][0:1, :]
        return state * cd + jnp.dot(kd_s[sl, :].astype(BF16), vb, preferred_element_type=F32)

    lax.fori_loop(0, S // C, scan, jnp.zeros((HEAD_DIM, HEAD_DIM), F32))

    z = z_ref[0].astype(F32)
    o_ref[0] = (_rms(o_s[...], on_ref[...]) * (z * _sigmoid(z))).astype(BF16)


def _gdn_masks():
    C = GDN_C
    i = np.arange(C)[:, None]
    j = np.arange(C)[None, :]
    mats = [(i >= j), (i == j)]
    s = 1
    while s < C:
        mats.append(((i // (2 * s)) == (j // (2 * s))) & ((i // s) % 2 == 1) & ((j // s) % 2 == 0))
        s *= 2
    return jnp.asarray(np.stack(mats).astype(np.float32))


def _gdn(p_main3, p_small3, conv_w, head_params, out_norm, masks):
    B, S, _ = p_main3.shape
    nm = masks.shape[0]
    cb = lambda base: (lambda b, h: (b, 0, base // LANES + h))
    cw = lambda base: (lambda b, h: (0, base // LANES + h))
    seq = lambda: pltpu.VMEM((S, LANES), F32)
    return pl.pallas_call(
        _gdn_kernel,
        grid=(B, HEADS),
        in_specs=[
            pl.BlockSpec((1, S, LANES), cb(C_GQ)),
            pl.BlockSpec((1, S, LANES), cb(C_GK)),
            pl.BlockSpec((1, S, LANES), cb(C_GV)),
            pl.BlockSpec((1, S, LANES), cb(C_GZ)),
            pl.BlockSpec((1, S, N_SMALL), lambda b, h: (b, 0, 0)),
            pl.BlockSpec((GDN_CONV, LANES), cw(0)),
            pl.BlockSpec((GDN_CONV, LANES), cw(BRANCH_W)),
            pl.BlockSpec((GDN_CONV, LANES), cw(2 * BRANCH_W)),
            pl.BlockSpec((1, 8, LANES), lambda b, h: (h, 0, 0)),
            pl.BlockSpec((1, LANES), lambda b, h: (0, 0)),
            pl.BlockSpec((nm, GDN_C, GDN_C), lambda b, h: (0, 0, 0)),
        ],
        out_specs=pl.BlockSpec((1, S, LANES), lambda b, h: (b, 0, h)),
        out_shape=jax.ShapeDtypeStruct((B, S, BRANCH_W), BF16),
        scratch_shapes=[seq() for _ in range(10)]
        + [pltpu.VMEM((S // GDN_C, 8, LANES), F32), seq()],
        compiler_params=_cparams(("parallel", "arbitrary")),
        name="gdn",
    )(p_main3, p_main3, p_main3, p_main3, p_small3, conv_w, conv_w, conv_w, head_params,
      out_norm, masks)


def _merge_kernel(of_ref, om_ref, og_ref, wb_ref, g0_ref, g1_ref, g2_ref, o_ref):
    acc = None
    for o, g, br in ((of_ref, g0_ref, 0), (om_ref, g1_ref, 1), (og_ref, g2_ref, 2)):
        y = _sigmoid(g[...].astype(F32)) * jnp.dot(o[...], wb_ref[br], preferred_element_type=F32)
        acc = y if acc is None else acc + y
    o_ref[...] = acc.astype(BF16)


def _merge(o_fox, o_mla, o_gdn, w_branch, p_main, tm=1024, tn=512):
    T = o_fox.shape[0]
    osp = pl.BlockSpec((tm, BRANCH_W), lambda i, j: (i, 0))
    gsp = lambda br: pl.BlockSpec((tm, tn), lambda i, j: (i, (C_GATE + br * D_MODEL) // tn + j))
    return pl.pallas_call(
        _merge_kernel,
        grid=(T // tm, D_MODEL // tn),
        in_specs=[osp, osp, osp,
                  pl.BlockSpec((3, BRANCH_W, tn), lambda i, j: (0, 0, j)),
                  gsp(0), gsp(1), gsp(2)],
        out_specs=pl.BlockSpec((tm, tn), lambda i, j: (i, j)),
        out_shape=jax.ShapeDtypeStruct((T, D_MODEL), BF16),
        compiler_params=_cparams(("parallel", "arbitrary")),
        name="merge",
    )(o_fox, o_mla, o_gdn, w_branch, p_main, p_main, p_main)


def _outproj_kernel(m_ref, w_ref, x_ref, o_ref):
    o_ref[...] = x_ref[...] + jnp.dot(m_ref[...], w_ref[...], preferred_element_type=F32)


def _outproj(merged, w_out, x2, tm=1024, tn=512):
    T = merged.shape[0]
    return pl.pallas_call(
        _outproj_kernel,
        grid=(T // tm, D_MODEL // tn),
        in_specs=[
            pl.BlockSpec((tm, D_MODEL), lambda i, j: (i, 0)),
            pl.BlockSpec((D_MODEL, tn), lambda i, j: (0, j)),
            pl.BlockSpec((tm, tn), lambda i, j: (i, j)),
        ],
        out_specs=pl.BlockSpec((tm, tn), lambda i, j: (i, j)),
        out_shape=jax.ShapeDtypeStruct((T, D_MODEL), F32),
        compiler_params=_cparams(("parallel", "arbitrary")),
        name="outproj",
    )(merged, w_out, x2)


def _ffn_kernel(x_ref, g_ref, wu_ref, wd_ref, fg_ref, o_ref, h_ref, *, final):
    f = pl.program_id(1)

    @pl.when(f == 0)
    def _():
        x = x_ref[...]
        h_ref[...] = _rms(x, g_ref[...]).astype(BF16)
        o_ref[...] = x

    hid = jnp.maximum(jnp.dot(h_ref[...], wu_ref[...], preferred_element_type=F32), 0.0)
    o_ref[...] += jnp.dot((hid * hid).astype(BF16), wd_ref[...], preferred_element_type=F32)

    if final:
        @pl.when(f == pl.num_programs(1) - 1)
        def _():
            o_ref[...] = _rms(o_ref[...], fg_ref[...])


def _ffn(x2, gain, w_up, w_down, final_gain, final, tm=512, tf=512):
    T = x2.shape[0]
    return pl.pallas_call(
        functools.partial(_ffn_kernel, final=final),
        grid=(T // tm, D_FF // tf),
        in_specs=[
            pl.BlockSpec((tm, D_MODEL), lambda i, f: (i, 0)),
            pl.BlockSpec((1, D_MODEL), lambda i, f: (0, 0)),
            pl.BlockSpec((D_MODEL, tf), lambda i, f: (0, f)),
            pl.BlockSpec((tf, D_MODEL), lambda i, f: (f, 0)),
            pl.BlockSpec((1, D_MODEL), lambda i, f: (0, 0)),
        ],
        out_specs=pl.BlockSpec((tm, D_MODEL), lambda i, f: (i, 0)),
        out_shape=jax.ShapeDtypeStruct((T, D_MODEL), F32),
        scratch_shapes=[pltpu.VMEM((tm, D_MODEL), BF16)],
        compiler_params=_cparams(("parallel", "arbitrary")),
        name="ffn",
    )(x2, gain, w_up, w_down, final_gain)


def _rope_tables(S):
    inv_freq = ROPE_THETA ** (-jnp.arange(0, MLA_ROPE, 2, dtype=F32) / MLA_ROPE)
    ang = jnp.arange(S, dtype=F32)[:, None] * inv_freq[None, :]
    cos, sin = jnp.cos(ang), jnp.sin(ang)
    zeros = jnp.zeros((S, LANES - MLA_ROPE), F32)
    return (jnp.concatenate([cos, cos, zeros], axis=1),
            jnp.concatenate([-sin, sin, zeros], axis=1))


def _swap_halves(w):
    half = MLA_ROPE // 2
    return jnp.concatenate([w[..., half:], w[..., :half]], axis=-1)


def _split_w_in(w):
    sizes = (BRANCH_W, BRANCH_W, BRANCH_W, HEADS, MLA_RANK, MLA_RANK, MLA_ROPE,
             3 * BRANCH_W, BRANCH_W, HEADS, HEADS, 3 * D_MODEL)
    offs = np.cumsum((0,) + sizes)
    seg = lambda n: w[:, offs[n]:offs[n + 1]]
    fq, fk, fv, ff, cq, ckv, kpe, gqkv, gz, gb, ga, gate = (seg(n) for n in range(12))
    main = jnp.concatenate([fq, fk, fv, cq, ckv, gqkv, gz, gate], axis=1).astype(BF16)
    z = lambda n: jnp.zeros((D_MODEL, n), F32)
    small = jnp.concatenate(
        [kpe, z(LANES - MLA_ROPE), _swap_halves(kpe), z(LANES - MLA_ROPE),
         ff, gb, ga, z(LANES - 3 * HEADS)], axis=1).astype(BF16)
    return main, small


def _split_mla(w_uq, w_ukv):
    nope, pe = w_uq[..., :HEAD_DIM], w_uq[..., HEAD_DIM:]
    zpad = jnp.zeros((MLA_RANK, HEADS, LANES - MLA_ROPE), F32)
    wqa = jnp.concatenate([nope, pe, zpad], axis=-1).reshape(MLA_RANK, HEADS * MLA_QK)
    wqb = jnp.concatenate([_swap_halves(pe), zpad], axis=-1).reshape(MLA_RANK, HEADS * LANES)
    wk = w_ukv[..., :HEAD_DIM].reshape(MLA_RANK, BRANCH_W)
    wv = w_ukv[..., HEAD_DIM:].reshape(MLA_RANK, BRANCH_W)
    return tuple(a.astype(BF16) for a in (wqa, wqb, wk, wv))


@jax.jit
def _forward(x, attn_norm, w_in, fox_fgate_bias, mla_q_norm, mla_kv_norm, w_mla_uq, w_mla_ukv,
             gdn_conv, gdn_a_log, gdn_dt_bias, gdn_out_norm, w_branch, w_out, mlp_norm,
             w_up, w_down, final_norm):
    B, S, D = x.shape
    T = B * S
    depth = w_in.shape[0]
    cos_t, sin_t = _rope_tables(S)
    masks = _gdn_masks()
    row = lambda v: v.reshape(1, -1).astype(F32)
    x2 = x.reshape(T, D)
    for l in range(depth):
        w_main, w_small = _split_w_in(w_in[l])
        p_main, p_small = _proj_in(x2, row(attn_norm[l]), w_main, w_small)
        pm3 = p_main.reshape(B, S, N_MAIN)
        ps3 = p_small.reshape(B, S, N_SMALL)

        bias_row = jnp.zeros((1, LANES), F32).at[0, S_FF:S_FF + HEADS].set(fox_fgate_bias[l])
        cum = _fox_cumsum(ps3, bias_row)
        o_fox = _attention(pm3, pm3, pm3, C_FQ // BRANCH_W, C_FK // BRANCH_W, C_FV // BRANCH_W,
                           HEAD_DIM, HEAD_DIM ** -0.5, cum=cum)

        wqa, wqb, wk, wv = _split_mla(w_mla_uq[l], w_mla_ukv[l])
        qf, kf, vf = _mla_prep(p_main, p_small, row(mla_q_norm[l]), row(mla_kv_norm[l]),
                               wqa, wqb, wk, wv, cos_t, sin_t, S)
        shp = lambda a: a.reshape(B, S, a.shape[-1])
        o_mla = _attention(shp(qf), shp(kf), shp(vf), 0, 0, 0, MLA_QK,
                           (HEAD_DIM + MLA_ROPE) ** -0.5)

        head_params = jnp.zeros((HEADS, 8, LANES), F32)
        head_params = head_params.at[:, 0, :].set(gdn_a_log[l][:, None])
        head_params = head_params.at[:, 1, :].set(gdn_dt_bias[l][:, None])
        o_gdn = _gdn(pm3, ps3, gdn_conv[l], head_params, row(gdn_out_norm[l]), masks)

        merged = _merge(o_fox.reshape(T, BRANCH_W), o_mla.reshape(T, BRANCH_W),
                        o_gdn.reshape(T, BRANCH_W), w_branch[l].astype(BF16), p_main)
        x2 = _outproj(merged, w_out[l].astype(BF16), x2)
        x2 = _ffn(x2, row(mlp_norm[l]), w_up[l].astype(BF16), w_down[l].astype(BF16),
                  row(final_norm), final=(l == depth - 1))
    return x2.reshape(B, S, D)


def kernel(x, attn_norm, w_in, fox_fgate_bias, mla_q_norm, mla_kv_norm, w_mla_uq, w_mla_ukv,
           gdn_conv, gdn_a_log, gdn_dt_bias, gdn_out_norm, w_branch, w_out, mlp_norm,
           w_up, w_down, final_norm):
    return _forward(x, attn_norm, w_in, fox_fgate_bias, mla_q_norm, mla_kv_norm, w_mla_uq,
                    w_mla_ukv, gdn_conv, gdn_a_log, gdn_dt_bias, gdn_out_norm, w_branch, w_out,
                    mlp_norm, w_up, w_down, final_norm)
```

```python
import functools
import math

import numpy as np
import jax
import jax.numpy as jnp
from jax import lax
from jax.experimental import pallas as pl
from jax.experimental.pallas import tpu as pltpu

F32 = jnp.float32
BF16 = jnp.bfloat16
HIGHEST = lax.Precision.HIGHEST

D_MODEL = 2048
HEAD_DIM = 128
HEADS = 8
BRANCH_W = HEADS * HEAD_DIM
MLA_RANK = 512
MLA_ROPE = 64
MLA_QK = 256
ROPE_THETA = 10000.0
GDN_CONV = 4
D_FF = 4 * D_MODEL
EPS = 1e-6

LANES = 128
VMEM_LIMIT = 52 * 1024 * 1024

C_FQ, C_FK, C_FV = 0, 1024, 2048
C_CQ, C_CKV = 3072, 3584
C_GQ, C_GK, C_GV, C_GZ = 4096, 5120, 6144, 7168
C_GATE = 8192
N_MAIN = C_GATE + 3 * D_MODEL
N_SMALL = 3 * LANES
S_FF, S_GB, S_GA = 0, 8, 16

GDN_C = 128
ATTN_GROUP = 4
GDN_GROUP = 4


def _cparams(sem):
    return pltpu.CompilerParams(dimension_semantics=sem, vmem_limit_bytes=VMEM_LIMIT)


def _sigmoid(x):
    return 1.0 / (1.0 + jnp.exp(-x))


def _softplus(x):
    return jnp.maximum(x, 0.0) + jnp.log(1.0 + jnp.exp(-jnp.abs(x)))


def _rms(x, gain):
    return x * lax.rsqrt(jnp.mean(x * x, axis=-1, keepdims=True) + EPS) * gain


def _proj_kernel(x_ref, g_ref, w_ref, ws_ref, o_ref, os_ref, u_ref):
    @pl.when(pl.program_id(1) == 0)
    def _():
        u = _rms(x_ref[...], g_ref[...]).astype(BF16)
        u_ref[...] = u
        os_ref[...] = jnp.dot(u, ws_ref[...], preferred_element_type=F32)

    o_ref[...] = jnp.dot(u_ref[...], w_ref[...], preferred_element_type=F32).astype(BF16)


def _proj_in(x2, gain, w_main, w_small, tm=1024, tn=512):
    T = x2.shape[0]
    return pl.pallas_call(
        _proj_kernel,
        grid=(T // tm, N_MAIN // tn),
        in_specs=[
            pl.BlockSpec((tm, D_MODEL), lambda i, j: (i, 0)),
            pl.BlockSpec((1, D_MODEL), lambda i, j: (0, 0)),
            pl.BlockSpec((D_MODEL, tn), lambda i, j: (0, j)),
            pl.BlockSpec((D_MODEL, N_SMALL), lambda i, j: (0, 0)),
        ],
        out_specs=[
            pl.BlockSpec((tm, tn), lambda i, j: (i, j)),
            pl.BlockSpec((tm, N_SMALL), lambda i, j: (i, 0)),
        ],
        out_shape=[
            jax.ShapeDtypeStruct((T, N_MAIN), BF16),
            jax.ShapeDtypeStruct((T, N_SMALL), F32),
        ],
        scratch_shapes=[pltpu.VMEM((tm, D_MODEL), BF16)],
        compiler_params=_cparams(("parallel", "arbitrary")),
        name="proj_in",
    )(x2, gain, w_main, w_small)


def _cumsum_kernel(s_ref, b_ref, c_ref, ct_ref):
    S = s_ref.shape[1]
    x = s_ref[0] + b_ref[...]
    logf = -_softplus(-x)
    r = lax.broadcasted_iota(jnp.int32, (LANES, LANES), 0)
    c = lax.broadcasted_iota(jnp.int32, (LANES, LANES), 1)
    tri = (r >= c).astype(F32)
    carry = jnp.zeros((1, LANES), F32)
    for blk in range(S // LANES):
        seg = logf[blk * LANES:(blk + 1) * LANES]
        cs = jnp.dot(tri, seg, precision=HIGHEST, preferred_element_type=F32) + carry
        c_ref[0, blk * LANES:(blk + 1) * LANES, :] = cs
        carry = cs[LANES - 1:LANES, :]
    ct_ref[0] = c_ref[0].T[:HEADS]


def _fox_cumsum(small3, bias_row):
    B, S, _ = small3.shape
    return pl.pallas_call(
        _cumsum_kernel,
        grid=(B,),
        in_specs=[
            pl.BlockSpec((1, S, LANES), lambda b: (b, 0, 2)),
            pl.BlockSpec((1, LANES), lambda b: (0, 0)),
        ],
        out_specs=[
            pl.BlockSpec((1, S, LANES), lambda b: (b, 0, 0)),
            pl.BlockSpec((1, HEADS, S), lambda b: (b, 0, 0)),
        ],
        out_shape=[
            jax.ShapeDtypeStruct((B, S, LANES), F32),
            jax.ShapeDtypeStruct((B, HEADS, S), F32),
        ],
        compiler_params=_cparams(("parallel",)),
        name="fox_cumsum",
    )(small3, bias_row)


def _attn_kernel(*refs, dk, scale, decay, tq):
    if decay:
        q_ref, k_ref, v_ref, cq_ref, ck_ref, o_ref = refs
    else:
        q_ref, k_ref, v_ref, o_ref = refs
    i = pl.program_id(1)
    row = lax.broadcasted_iota(jnp.int32, (tq, tq), 0)
    col = lax.broadcasted_iota(jnp.int32, (tq, tq), 1)
    causal = col <= row
    def group_step(hs, qs, cqs, j, carry, masked):
        off = pl.multiple_of(j * tq, tq)
        ss = []
        for h, q, cq in zip(hs, qs, cqs):
            k = k_ref[0, pl.ds(off, tq), h * dk:(h + 1) * dk]
            s = lax.dot_general(q, k, (((1,), (1,)), ((), ())), preferred_element_type=F32) * scale
            if decay:
                s = s + cq - ck_ref[0, h:h + 1, pl.ds(off, tq)]
            if masked:
                s = jnp.where(causal, s, -jnp.inf)
            ss.append(s)
        ps = []
        for s, (m, l, acc) in zip(ss, carry):
            m_new = jnp.maximum(m, jnp.max(s, axis=-1, keepdims=True))
            p = jnp.exp(s - m_new)
            alpha = jnp.exp(m - m_new)
            ps.append((m_new, alpha, alpha * l + jnp.sum(p, axis=-1, keepdims=True), p.astype(BF16)))
        out = []
        for h, (m_new, alpha, l, p), (_, _, acc) in zip(hs, ps, carry):
            v = v_ref[0, pl.ds(off, tq), h * HEAD_DIM:(h + 1) * HEAD_DIM]
            out.append((m_new, l, alpha * acc + jnp.dot(p, v, preferred_element_type=F32)))
        return tuple(out)

    for h0 in range(0, HEADS, ATTN_GROUP):
        hs = tuple(range(h0, h0 + ATTN_GROUP))
        qs = [q_ref[0, :, h * dk:(h + 1) * dk] for h in hs]
        cqs = [cq_ref[0, :, h:h + 1] if decay else None for h in hs]

        def step(j, carry, masked, hs=hs, qs=qs, cqs=cqs):
            return group_step(hs, qs, cqs, j, carry, masked)

        init = tuple((jnp.full((tq, 1), -jnp.inf, F32), jnp.zeros((tq, 1), F32),
                      jnp.zeros((tq, HEAD_DIM), F32)) for _ in hs)
        carry = lax.fori_loop(0, i, functools.partial(step, masked=False), init)
        for h, (_, l, acc) in zip(hs, step(i, carry, masked=True)):
            o_ref[0, :, h * HEAD_DIM:(h + 1) * HEAD_DIM] = (acc / l).astype(BF16)


def _attention(q_arr, k_arr, v_arr, q_blk, k_blk, v_blk, dk, scale, cum=None, tq=256):
    B, S, _ = q_arr.shape
    decay = cum is not None
    in_specs = [
        pl.BlockSpec((1, tq, HEADS * dk), lambda b, i: (b, i, q_blk)),
        pl.BlockSpec((1, S, HEADS * dk), lambda b, i: (b, 0, k_blk)),
        pl.BlockSpec((1, S, BRANCH_W), lambda b, i: (b, 0, v_blk)),
    ]
    args = [q_arr, k_arr, v_arr]
    if decay:
        in_specs += [
            pl.BlockSpec((1, tq, LANES), lambda b, i: (b, i, 0)),
            pl.BlockSpec((1, HEADS, S), lambda b, i: (b, 0, 0)),
        ]
        args += list(cum)
    return pl.pallas_call(
        functools.partial(_attn_kernel, dk=dk, scale=scale, decay=decay, tq=tq),
        grid=(B, S // tq),
        in_specs=in_specs,
        out_specs=pl.BlockSpec((1, tq, BRANCH_W), lambda b, i: (b, i, 0)),
        out_shape=jax.ShapeDtypeStruct((B, S, BRANCH_W), BF16),
        compiler_params=_cparams(("parallel", "arbitrary")),
        name="fox_attn" if decay else "mla_attn",
    )(*args)


def _mla_prep_kernel(cq_ref, ckv_ref, sm_ref, qn_ref, kvn_ref, wqa_ref, wqb_ref, wk_ref, wv_ref,
                     cos_ref, sin_ref, q_ref, k_ref, v_ref):
    cos = cos_ref[...]
    sin = sin_ref[...]
    qn = _rms(cq_ref[...].astype(F32), qn_ref[...]).astype(BF16)
    a = jnp.dot(qn, wqa_ref[...], preferred_element_type=F32)
    bsw = jnp.dot(qn, wqb_ref[...], preferred_element_type=F32)
    kvn = _rms(ckv_ref[...].astype(F32), kvn_ref[...]).astype(BF16)
    kn = jnp.dot(kvn, wk_ref[...], preferred_element_type=F32)
    v_ref[...] = jnp.dot(kvn, wv_ref[...], preferred_element_type=F32).astype(BF16)
    sm = sm_ref[...]
    kpe = (sm[:, 0:LANES] * cos + sm[:, LANES:2 * LANES] * sin).astype(BF16)
    for h in range(HEADS):
        lo = h * MLA_QK
        q_ref[:, lo:lo + LANES] = a[:, lo:lo + LANES].astype(BF16)
        q_ref[:, lo + LANES:lo + 2 * LANES] = (
            a[:, lo + LANES:lo + 2 * LANES] * cos + bsw[:, h * LANES:(h + 1) * LANES] * sin
        ).astype(BF16)
        k_ref[:, lo:lo + LANES] = kn[:, h * LANES:(h + 1) * LANES].astype(BF16)
        k_ref[:, lo + LANES:lo + 2 * LANES] = kpe


def _mla_prep(p_main, p_small, qn, kvn, wqa, wqb, wk, wv, cos_t, sin_t, S, tm=512):
    T = p_main.shape[0]
    nS = S // tm
    const = lambda i: (0, 0)
    return pl.pallas_call(
        _mla_prep_kernel,
        grid=(T // tm,),
        in_specs=[
            pl.BlockSpec((tm, MLA_RANK), lambda i: (i, C_CQ // MLA_RANK)),
            pl.BlockSpec((tm, MLA_RANK), lambda i: (i, C_CKV // MLA_RANK)),
            pl.BlockSpec((tm, N_SMALL), lambda i: (i, 0)),
            pl.BlockSpec((1, MLA_RANK), const),
            pl.BlockSpec((1, MLA_RANK), const),
            pl.BlockSpec((MLA_RANK, HEADS * MLA_QK), const),
            pl.BlockSpec((MLA_RANK, HEADS * LANES), const),
            pl.BlockSpec((MLA_RANK, BRANCH_W), const),
            pl.BlockSpec((MLA_RANK, BRANCH_W), const),
            pl.BlockSpec((tm, LANES), lambda i: (i % nS, 0)),
            pl.BlockSpec((tm, LANES), lambda i: (i % nS, 0)),
        ],
        out_specs=[
            pl.BlockSpec((tm, HEADS * MLA_QK), lambda i: (i, 0)),
            pl.BlockSpec((tm, HEADS * MLA_QK), lambda i: (i, 0)),
            pl.BlockSpec((tm, BRANCH_W), lambda i: (i, 0)),
        ],
        out_shape=[
            jax.ShapeDtypeStruct((T, HEADS * MLA_QK), BF16),
            jax.ShapeDtypeStruct((T, HEADS * MLA_QK), BF16),
            jax.ShapeDtypeStruct((T, BRANCH_W), BF16),
        ],
        compiler_params=_cparams(("parallel",)),
        name="mla_prep",
    )(p_main, p_main, p_small, qn, kvn, wqa, wqb, wk, wv, cos_t, sin_t)


def _gdn_kernel(q_ref, k_ref, v_ref, z_ref, sm_ref, cwq_ref, cwk_ref, cwv_ref, hp_ref, on_ref,
                mk_ref, o_ref, qn_s, kn_s, vn_s, g_s, b_s, u_s, w_s, a_s, qd_s, kd_s, cd_s, o_s):
    h = pl.program_id(1)
    S = q_ref.shape[1]
    C = GDN_C
    rows = lax.broadcasted_iota(jnp.int32, (S, LANES), 0)
    lane = lax.broadcasted_iota(jnp.int32, (S, LANES), 1)

    def conv_silu(x_ref, cw_ref):
        x = x_ref[0].astype(F32)
        w = cw_ref[...]
        y = x * w[GDN_CONV - 1:GDN_CONV]
        for sft in range(1, GDN_CONV):
            xs = jnp.where(rows >= sft, pltpu.roll(x, sft, 0), 0.0)
            y = y + xs * w[GDN_CONV - 1 - sft:GDN_CONV - sft]
        return y * _sigmoid(y)

    def l2n(x):
        return x * lax.rsqrt(jnp.sum(x * x, axis=-1, keepdims=True) + EPS)

    qn_s[...] = l2n(conv_silu(q_ref, cwq_ref)) * (HEAD_DIM ** -0.5)
    kn_s[...] = l2n(conv_silu(k_ref, cwk_ref))
    vn_s[...] = conv_silu(v_ref, cwv_ref)

    seg = sm_ref[0, :, 2 * LANES:3 * LANES]
    b_logit = jnp.sum(jnp.where(lane == S_GB + h, seg, 0.0), axis=-1, keepdims=True)
    a_logit = jnp.sum(jnp.where(lane == S_GA + h, seg, 0.0), axis=-1, keepdims=True)
    a_log = hp_ref[0, 0:1, :]
    dt_b = hp_ref[0, 1:2, :]
    b_s[...] = jnp.broadcast_to(_sigmoid(b_logit), (S, LANES))
    g_s[...] = -jnp.exp(a_log) * _softplus(a_logit + dt_b)

    tri = mk_ref[0]
    eye = mk_ref[1]
    incl = tri > 0.5
    strict = tri - eye

    nt = (((1,), (1,)), ((), ()))
    dot = functools.partial(jnp.dot, preferred_element_type=F32)

    def local(n, _):
        G = range(GDN_GROUP)
        sls = [pl.ds(pl.multiple_of((n * GDN_GROUP + g) * C, C), C) for g in G]
        q = [qn_s[sl, :] for sl in sls]
        k = [kn_s[sl, :] for sl in sls]
        beta = [b_s[sl, :] for sl in sls]
        gc = [jnp.dot(tri, g_s[sl, :], precision=HIGHEST, preferred_element_type=F32) for sl in sls]
        decay = [jnp.exp(jnp.where(incl, x - x.T, -jnp.inf)) for x in gc]
        kb = [k[g] * beta[g] for g in G]
        kbf = [x.astype(BF16) for x in k]
        a = [lax.dot_general(kb[g].astype(BF16), kbf[g], nt, preferred_element_type=F32)
             * decay[g] * strict for g in G]
        for g in G:
            a_s[sls[g], :] = lax.dot_general(q[g].astype(BF16), kbf[g], nt,
                                             preferred_element_type=F32) * decay[g]
        t = [eye - a[g] * mk_ref[2] for g in G]
        for lvl in range(1, int(math.log2(C))):
            tb = [x.astype(BF16) for x in t]
            tc = [dot(tb[g], (a[g] * mk_ref[2 + lvl]).astype(BF16)).astype(BF16) for g in G]
            t = [t[g] - dot(tc[g], tb[g]) for g in G]
        tb = [x.astype(BF16) for x in t]
        eg = [jnp.exp(x) for x in gc]
        for g in G:
            u_s[sls[g], :] = dot(tb[g], (vn_s[sls[g], :] * beta[g]).astype(BF16))
        for g in G:
            w_s[sls[g], :] = dot(tb[g], (kb[g] * eg[g]).astype(BF16))
        for g in G:
            glast = gc[g][C - 1:C, :]
            qd_s[sls[g], :] = q[g] * eg[g]
            kd_s[sls[g], :] = (k[g] * jnp.exp(glast - gc[g])).T
            cd_s[n * GDN_GROUP + g] = jnp.broadcast_to(jnp.exp(glast), (8, LANES))
        return 0

    lax.fori_loop(0, S // C // GDN_GROUP, local, 0)

    def scan(n, state):
        sl = pl.ds(pl.multiple_of(n * C, C), C)
        sb = state.astype(BF16)
        vnew = u_s[sl, :] - jnp.dot(w_s[sl, :].astype(BF16), sb, preferred_element_type=F32)
        vb = vnew.astype(BF16)
        o_s[sl, :] = (jnp.dot(qd_s[sl, :].astype(BF16), sb, preferred_element_type=F32)
                      + jnp.dot(a_s[sl, :].astype(BF16), vb, preferred_element_type=F32))
        cd = cd_s[n][0:1, :]
        return state * cd + jnp.dot(kd_s[sl, :].astype(BF16), vb, preferred_element_type=F32)

    lax.fori_loop(0, S // C, scan, jnp.zeros((HEAD_DIM, HEAD_DIM), F32))

    z = z_ref[0].astype(F32)
    o_ref[0] = (_rms(o_s[...], on_ref[...]) * (z * _sigmoid(z))).astype(BF16)


def _gdn_masks():
    C = GDN_C
    i = np.arange(C)[:, None]
    j = np.arange(C)[None, :]
    mats = [(i >= j), (i == j)]
    s = 1
    while s < C:
        mats.append(((i // (2 * s)) == (j // (2 * s))) & ((i // s) % 2 == 1) & ((j // s) % 2 == 0))
        s *= 2
    return jnp.asarray(np.stack(mats).astype(np.float32))


def _gdn(p_main3, p_small3, conv_w, head_params, out_norm, masks):
    B, S, _ = p_main3.shape
    nm = masks.shape[0]
    cb = lambda base: (lambda b, h: (b, 0, base // LANES + h))
    cw = lambda base: (lambda b, h: (0, base // LANES + h))
    seq = lambda: pltpu.VMEM((S, LANES), F32)
    return pl.pallas_call(
        _gdn_kernel,
        grid=(B, HEADS),
        in_specs=[
            pl.BlockSpec((1, S, LANES), cb(C_GQ)),
            pl.BlockSpec((1, S, LANES), cb(C_GK)),
            pl.BlockSpec((1, S, LANES), cb(C_GV)),
            pl.BlockSpec((1, S, LANES), cb(C_GZ)),
            pl.BlockSpec((1, S, N_SMALL), lambda b, h: (b, 0, 0)),
            pl.BlockSpec((GDN_CONV, LANES), cw(0)),
            pl.BlockSpec((GDN_CONV, LANES), cw(BRANCH_W)),
            pl.BlockSpec((GDN_CONV, LANES), cw(2 * BRANCH_W)),
            pl.BlockSpec((1, 8, LANES), lambda b, h: (h, 0, 0)),
            pl.BlockSpec((1, LANES), lambda b, h: (0, 0)),
            pl.BlockSpec((nm, GDN_C, GDN_C), lambda b, h: (0, 0, 0)),
        ],
        out_specs=pl.BlockSpec((1, S, LANES), lambda b, h: (b, 0, h)),
        out_shape=jax.ShapeDtypeStruct((B, S, BRANCH_W), BF16),
        scratch_shapes=[seq() for _ in range(10)]
        + [pltpu.VMEM((S // GDN_C, 8, LANES), F32), seq()],
        compiler_params=_cparams(("parallel", "arbitrary")),
        name="gdn",
    )(p_main3, p_main3, p_main3, p_main3, p_small3, conv_w, conv_w, conv_w, head_params,
      out_norm, masks)


def _merge_kernel(of_ref, om_ref, og_ref, wb_ref, g0_ref, g1_ref, g2_ref, o_ref):
    acc = None
    for o, g, br in ((of_ref, g0_ref, 0), (om_ref, g1_ref, 1), (og_ref, g2_ref, 2)):
        y = _sigmoid(g[...].astype(F32)) * jnp.dot(o[...], wb_ref[br], preferred_element_type=F32)
        acc = y if acc is None else acc + y
    o_ref[...] = acc.astype(BF16)


def _merge(o_fox, o_mla, o_gdn, w_branch, p_main, tm=1024, tn=512):
    T = o_fox.shape[0]
    osp = pl.BlockSpec((tm, BRANCH_W), lambda i, j: (i, 0))
    gsp = lambda br: pl.BlockSpec((tm, tn), lambda i, j: (i, (C_GATE + br * D_MODEL) // tn + j))
    return pl.pallas_call(
        _merge_kernel,
        grid=(T // tm, D_MODEL // tn),
        in_specs=[osp, osp, osp,
                  pl.BlockSpec((3, BRANCH_W, tn), lambda i, j: (0, 0, j)),
                  gsp(0), gsp(1), gsp(2)],
        out_specs=pl.BlockSpec((tm, tn), lambda i, j: (i, j)),
        out_shape=jax.ShapeDtypeStruct((T, D_MODEL), BF16),
        compiler_params=_cparams(("parallel", "arbitrary")),
        name="merge",
    )(o_fox, o_mla, o_gdn, w_branch, p_main, p_main, p_main)


def _outproj_kernel(m_ref, w_ref, x_ref, o_ref):
    o_ref[...] = x_ref[...] + jnp.dot(m_ref[...], w_ref[...], preferred_element_type=F32)


def _outproj(merged, w_out, x2, tm=1024, tn=512):
    T = merged.shape[0]
    return pl.pallas_call(
        _outproj_kernel,
        grid=(T // tm, D_MODEL // tn),
        in_specs=[
            pl.BlockSpec((tm, D_MODEL), lambda i, j: (i, 0)),
            pl.BlockSpec((D_MODEL, tn), lambda i, j: (0, j)),
            pl.BlockSpec((tm, tn), lambda i, j: (i, j)),
        ],
        out_specs=pl.BlockSpec((tm, tn), lambda i, j: (i, j)),
        out_shape=jax.ShapeDtypeStruct((T, D_MODEL), F32),
        compiler_params=_cparams(("parallel", "arbitrary")),
        name="outproj",
    )(merged, w_out, x2)


def _ffn_kernel(x_ref, g_ref, wu_ref, wd_ref, fg_ref, o_ref, h_ref, *, final):
    f = pl.program_id(1)

    @pl.when(f == 0)
    def _():
        x = x_ref[...]
        h_ref[...] = _rms(x, g_ref[...]).astype(BF16)
        o_ref[...] = x

    hid = jnp.maximum(jnp.dot(h_ref[...], wu_ref[...], preferred_element_type=F32), 0.0)
    o_ref[...] += jnp.dot((hid * hid).astype(BF16), wd_ref[...], preferred_element_type=F32)

    if final:
        @pl.when(f == pl.num_programs(1) - 1)
        def _():
            o_ref[...] = _rms(o_ref[...], fg_ref[...])


def _ffn(x2, gain, w_up, w_down, final_gain, final, tm=512, tf=512):
    T = x2.shape[0]
    return pl.pallas_call(
        functools.partial(_ffn_kernel, final=final),
        grid=(T // tm, D_FF // tf),
        in_specs=[
            pl.BlockSpec((tm, D_MODEL), lambda i, f: (i, 0)),
            pl.BlockSpec((1, D_MODEL), lambda i, f: (0, 0)),
            pl.BlockSpec((D_MODEL, tf), lambda i, f: (0, f)),
            pl.BlockSpec((tf, D_MODEL), lambda i, f: (f, 0)),
            pl.BlockSpec((1, D_MODEL), lambda i, f: (0, 0)),
        ],
        out_specs=pl.BlockSpec((tm, D_MODEL), lambda i, f: (i, 0)),
        out_shape=jax.ShapeDtypeStruct((T, D_MODEL), F32),
        scratch_shapes=[pltpu.VMEM((tm, D_MODEL), BF16)],
        compiler_params=_cparams(("parallel", "arbitrary")),
        name="ffn",
    )(x2, gain, w_up, w_down, final_gain)


def _rope_tables(S):
    inv_freq = ROPE_THETA ** (-jnp.arange(0, MLA_ROPE, 2, dtype=F32) / MLA_ROPE)
    ang = jnp.arange(S, dtype=F32)[:, None] * inv_freq[None, :]
    cos, sin = jnp.cos(ang), jnp.sin(ang)
    zeros = jnp.zeros((S, LANES - MLA_ROPE), F32)
    return (jnp.concatenate([cos, cos, zeros], axis=1),
            jnp.concatenate([-sin, sin, zeros], axis=1))


def _swap_halves(w):
    half = MLA_ROPE // 2
    return jnp.concatenate([w[..., half:], w[..., :half]], axis=-1)


def _split_w_in(w):
    sizes = (BRANCH_W, BRANCH_W, BRANCH_W, HEADS, MLA_RANK, MLA_RANK, MLA_ROPE,
             3 * BRANCH_W, BRANCH_W, HEADS, HEADS, 3 * D_MODEL)
    offs = np.cumsum((0,) + sizes)
    seg = lambda n: w[:, offs[n]:offs[n + 1]]
    fq, fk, fv, ff, cq, ckv, kpe, gqkv, gz, gb, ga, gate = (seg(n) for n in range(12))
    main = jnp.concatenate([fq, fk, fv, cq, ckv, gqkv, gz, gate], axis=1).astype(BF16)
    z = lambda n: jnp.zeros((D_MODEL, n), F32)
    small = jnp.concatenate(
        [kpe, z(LANES - MLA_ROPE), _swap_halves(kpe), z(LANES - MLA_ROPE),
         ff, gb, ga, z(LANES - 3 * HEADS)], axis=1).astype(BF16)
    return main, small


def _split_mla(w_uq, w_ukv):
    nope, pe = w_uq[..., :HEAD_DIM], w_uq[..., HEAD_DIM:]
    zpad = jnp.zeros((MLA_RANK, HEADS, LANES - MLA_ROPE), F32)
    wqa = jnp.concatenate([nope, pe, zpad], axis=-1).reshape(MLA_RANK, HEADS * MLA_QK)
    wqb = jnp.concatenate([_swap_halves(pe), zpad], axis=-1).reshape(MLA_RANK, HEADS * LANES)
    wk = w_ukv[..., :HEAD_DIM].reshape(MLA_RANK, BRANCH_W)
    wv = w_ukv[..., HEAD_DIM:].reshape(MLA_RANK, BRANCH_W)
    return tuple(a.astype(BF16) for a in (wqa, wqb, wk, wv))


@jax.jit
def _forward(x, attn_norm, w_in, fox_fgate_bias, mla_q_norm, mla_kv_norm, w_mla_uq, w_mla_ukv,
             gdn_conv, gdn_a_log, gdn_dt_bias, gdn_out_norm, w_branch, w_out, mlp_norm,
             w_up, w_down, final_norm):
    B, S, D = x.shape
    T = B * S
    depth = w_in.shape[0]
    cos_t, sin_t = _rope_tables(S)
    masks = _gdn_masks()
    row = lambda v: v.reshape(1, -1).astype(F32)
    x2 = x.reshape(T, D)
    for l in range(depth):
        w_main, w_small = _split_w_in(w_in[l])
        p_main, p_small = _proj_in(x2, row(attn_norm[l]), w_main, w_small)
        pm3 = p_main.reshape(B, S, N_MAIN)
        ps3 = p_small.reshape(B, S, N_SMALL)

        bias_row = jnp.zeros((1, LANES), F32).at[0, S_FF:S_FF + HEADS].set(fox_fgate_bias[l])
        cum = _fox_cumsum(ps3, bias_row)
        o_fox = _attention(pm3, pm3, pm3, C_FQ // BRANCH_W, C_FK // BRANCH_W, C_FV // BRANCH_W,
                           HEAD_DIM, HEAD_DIM ** -0.5, cum=cum)

        wqa, wqb, wk, wv = _split_mla(w_mla_uq[l], w_mla_ukv[l])
        qf, kf, vf = _mla_prep(p_main, p_small, row(mla_q_norm[l]), row(mla_kv_norm[l]),
                               wqa, wqb, wk, wv, cos_t, sin_t, S)
        shp = lambda a: a.reshape(B, S, a.shape[-1])
        o_mla = _attention(shp(qf), shp(kf), shp(vf), 0, 0, 0, MLA_QK,
                           (HEAD_DIM + MLA_ROPE) ** -0.5)

        head_params = jnp.zeros((HEADS, 8, LANES), F32)
        head_params = head_params.at[:, 0, :].set(gdn_a_log[l][:, None])
        head_params = head_params.at[:, 1, :].set(gdn_dt_bias[l][:, None])
        o_gdn = _gdn(pm3, ps3, gdn_conv[l], head_params, row(gdn_out_norm[l]), masks)

        merged = _merge(o_fox.reshape(T, BRANCH_W), o_mla.reshape(T, BRANCH_W),
                        o_gdn.reshape(T, BRANCH_W), w_branch[l].astype(BF16), p_main)
        x2 = _outproj(merged, w_out[l].astype(BF16), x2)
        x2 = _ffn(x2, row(mlp_norm[l]), w_up[l].astype(BF16), w_down[l].astype(BF16),
                  row(final_norm), final=(l == depth - 1))
    return x2.reshape(B, S, D)


def kernel(x, attn_norm, w_in, fox_fgate_bias, mla_q_norm, mla_kv_norm, w_mla_uq, w_mla_ukv,
           gdn_conv, gdn_a_log, gdn_dt_bias, gdn_out_norm, w_branch, w_out, mlp_norm,
           w_up, w_down, final_norm):
    return _forward(x, attn_norm, w_in, fox_fgate_bias, mla_q_norm, mla_kv_norm, w_mla_uq,
                    w_mla_ukv, gdn_conv, gdn_a_log, gdn_dt_bias, gdn_out_norm, w_branch, w_out,
                    mlp_norm, w_up, w_down, final_norm)
```

```python
import functools
import math

import numpy as np
import jax
import jax.numpy as jnp
from jax import lax
from jax.experimental import pallas as pl
from jax.experimental.pallas import tpu as pltpu

F32 = jnp.float32
BF16 = jnp.bfloat16
HIGHEST = lax.Precision.HIGHEST

D_MODEL = 2048
HEAD_DIM = 128
HEADS = 8
BRANCH_W = HEADS * HEAD_DIM
MLA_RANK = 512
MLA_ROPE = 64
MLA_QK = 256
ROPE_THETA = 10000.0
GDN_CONV = 4
D_FF = 4 * D_MODEL
EPS = 1e-6
LOG2E = math.log2(math.e)

LANES = 128
VMEM_LIMIT = 52 * 1024 * 1024

C_FQ, C_FK, C_FV = 0, 1024, 2048
C_CQ, C_CKV = 3072, 3584
C_GQ, C_GK, C_GV, C_GZ = 4096, 5120, 6144, 7168
C_GATE = 8192
N_MAIN = C_GATE + 3 * D_MODEL
N_SMALL = 3 * LANES
S_FF, S_GB, S_GA = 0, 8, 16

GDN_C = 128
ATTN_GROUP = 8
GDN_GROUP = 8


def _cparams(sem):
    return pltpu.CompilerParams(dimension_semantics=sem, vmem_limit_bytes=VMEM_LIMIT)


def _sigmoid(x):
    return 0.5 * jnp.tanh(0.5 * x) + 0.5


def _softplus(x):
    return jnp.maximum(x, 0.0) + jnp.log(1.0 + jnp.exp(-jnp.abs(x)))


def _rms(x, gain):
    return x * lax.rsqrt(jnp.mean(x * x, axis=-1, keepdims=True) + EPS) * gain


def _proj_kernel(x_ref, g_ref, w_ref, ws_ref, o_ref, os_ref, u_ref):
    @pl.when(pl.program_id(1) == 0)
    def _():
        u = _rms(x_ref[...], g_ref[...]).astype(BF16)
        u_ref[...] = u
        os_ref[...] = jnp.dot(u, ws_ref[...], preferred_element_type=F32)

    o_ref[...] = jnp.dot(u_ref[...], w_ref[...], preferred_element_type=F32).astype(BF16)


def _proj_in(x2, gain, w_main, w_small, tm=1024, tn=1024):
    T = x2.shape[0]
    return pl.pallas_call(
        _proj_kernel,
        grid=(T // tm, N_MAIN // tn),
        in_specs=[
            pl.BlockSpec((tm, D_MODEL), lambda i, j: (i, 0), pipeline_mode=pl.Buffered(1)),
            pl.BlockSpec((1, D_MODEL), lambda i, j: (0, 0)),
            pl.BlockSpec((D_MODEL, tn), lambda i, j: (0, j)),
            pl.BlockSpec((D_MODEL, N_SMALL), lambda i, j: (0, 0)),
        ],
        out_specs=[
            pl.BlockSpec((tm, tn), lambda i, j: (i, j)),
            pl.BlockSpec((tm, N_SMALL), lambda i, j: (i, 0)),
        ],
        out_shape=[
            jax.ShapeDtypeStruct((T, N_MAIN), BF16),
            jax.ShapeDtypeStruct((T, N_SMALL), F32),
        ],
        scratch_shapes=[pltpu.VMEM((tm, D_MODEL), BF16)],
        compiler_params=_cparams(("parallel", "arbitrary")),
        name="proj_in",
    )(x2, gain, w_main, w_small)


def _cumsum_kernel(s_ref, b_ref, al_ref, c_ref, ct_ref, gt_ref):
    S = s_ref.shape[1]
    x = s_ref[0] + b_ref[...]
    lane = lax.broadcasted_iota(jnp.int32, (S, LANES), 1)
    gt_ref[0] = jnp.where(lane < S_GA, _sigmoid(x), -jnp.exp(al_ref[...]) * _softplus(x))
    logf = -_softplus(-x) * LOG2E
    r = lax.broadcasted_iota(jnp.int32, (LANES, LANES), 0)
    c = lax.broadcasted_iota(jnp.int32, (LANES, LANES), 1)
    tri = (r >= c).astype(F32)
    carry = jnp.zeros((1, LANES), F32)
    for blk in range(S // LANES):
        seg = logf[blk * LANES:(blk + 1) * LANES]
        cs = jnp.dot(tri, seg, precision=HIGHEST, preferred_element_type=F32) + carry
        c_ref[0, blk * LANES:(blk + 1) * LANES, :] = cs
        carry = cs[LANES - 1:LANES, :]
    ct_ref[0] = c_ref[0].T[:HEADS]


def _gates(small3, bias_row, alog_row):
    B, S, _ = small3.shape
    return pl.pallas_call(
        _cumsum_kernel,
        grid=(B,),
        in_specs=[
            pl.BlockSpec((1, S, LANES), lambda b: (b, 0, 2)),
            pl.BlockSpec((1, LANES), lambda b: (0, 0)),
            pl.BlockSpec((1, LANES), lambda b: (0, 0)),
        ],
        out_specs=[
            pl.BlockSpec((1, S, LANES), lambda b: (b, 0, 0)),
            pl.BlockSpec((1, HEADS, S), lambda b: (b, 0, 0)),
            pl.BlockSpec((1, S, LANES), lambda b: (b, 0, 0)),
        ],
        out_shape=[
            jax.ShapeDtypeStruct((B, S, LANES), F32),
            jax.ShapeDtypeStruct((B, HEADS, S), F32),
            jax.ShapeDtypeStruct((B, S, LANES), F32),
        ],
        compiler_params=_cparams(("parallel",)),
        name="gates",
    )(small3, bias_row, alog_row)


def _attn_kernel(*refs, dk, scale, decay, tq):
    if decay:
        q_ref, k_ref, vt_ref, cq_ref, ck_ref, o_ref = refs
    else:
        q_ref, k_ref, vt_ref, o_ref = refs
    i = pl.program_id(1)
    krow = lax.broadcasted_iota(jnp.int32, (tq, tq), 0)
    qcol = lax.broadcasted_iota(jnp.int32, (tq, tq), 1)
    causal = krow <= qcol
    c2 = scale * LOG2E

    def group_step(hs, qs, cqs, j, carry, masked):
        off = pl.multiple_of(j * tq, tq)
        ts = []
        for h, q in zip(hs, qs):
            k = k_ref[0, pl.ds(off, tq), h * dk:(h + 1) * dk]
            t = lax.dot_general(k, q, (((1,), (1,)), ((), ())), preferred_element_type=F32) * c2
            if decay:
                t = t - ck_ref[0, pl.ds(off, tq), h:h + 1]
            if masked:
                t = jnp.where(causal, t, -jnp.inf)
            ts.append(t)
        ps = []
        for t, cq, (m, l, acc) in zip(ts, cqs, carry):
            mt = jnp.max(t, axis=0, keepdims=True)
            m_new = jnp.maximum(m, mt + cq if decay else mt)
            p = jnp.exp2(t - (m_new - cq if decay else m_new))
            alpha = jnp.exp2(m - m_new)
            ps.append((m_new, alpha, alpha * l + jnp.sum(p, axis=0, keepdims=True), p.astype(BF16)))
        out = []
        for h, (m_new, alpha, l, p), (_, _, acc) in zip(hs, ps, carry):
            vt = vt_ref[0, h * HEAD_DIM:(h + 1) * HEAD_DIM, pl.ds(off, tq)]
            out.append((m_new, l, alpha * acc + jnp.dot(vt, p, preferred_element_type=F32)))
        return tuple(out)

    for h0 in range(0, HEADS, ATTN_GROUP):
        hs = tuple(range(h0, h0 + ATTN_GROUP))
        qs = [q_ref[0, :, h * dk:(h + 1) * dk] for h in hs]
        cqs = [cq_ref[0, h:h + 1, :] if decay else None for h in hs]

        def step(j, carry, masked, hs=hs, qs=qs, cqs=cqs):
            return group_step(hs, qs, cqs, j, carry, masked)

        init = tuple((jnp.full((1, tq), -jnp.inf, F32), jnp.zeros((1, tq), F32),
                      jnp.zeros((HEAD_DIM, tq), F32)) for _ in hs)
        carry = lax.fori_loop(0, i, functools.partial(step, masked=False), init)
        for h, (_, l, acc) in zip(hs, step(i, carry, masked=True)):
            o_ref[0, :, h * HEAD_DIM:(h + 1) * HEAD_DIM] = (acc / l).T.astype(BF16)


def _attention(q_arr, k_arr, vt_arr, q_blk, k_blk, dk, scale, cum=None, tq=256):
    B, S, _ = q_arr.shape
    decay = cum is not None
    in_specs = [
        pl.BlockSpec((1, tq, HEADS * dk), lambda b, i: (b, i, q_blk)),
        pl.BlockSpec((1, S, HEADS * dk), lambda b, i: (b, 0, k_blk)),
        pl.BlockSpec((1, BRANCH_W, S), lambda b, i: (b, 0, 0)),
    ]
    args = [q_arr, k_arr, vt_arr]
    if decay:
        in_specs += [
            pl.BlockSpec((1, HEADS, tq), lambda b, i: (b, 0, i)),
            pl.BlockSpec((1, S, LANES), lambda b, i: (b, 0, 0)),
        ]
        args += [cum[1], cum[0]]
    return pl.pallas_call(
        functools.partial(_attn_kernel, dk=dk, scale=scale, decay=decay, tq=tq),
        grid=(B, S // tq),
        in_specs=in_specs,
        out_specs=pl.BlockSpec((1, tq, BRANCH_W), lambda b, i: (b, i, 0)),
        out_shape=jax.ShapeDtypeStruct((B, S, BRANCH_W), BF16),
        compiler_params=_cparams(("parallel", "arbitrary")),
        name="fox_attn" if decay else "mla_attn",
    )(*args)


def _mla_prep_kernel(cq_ref, ckv_ref, sm_ref, qn_ref, kvn_ref, wqa_ref, wqb_ref, wk_ref, wv_ref,
                     cos_ref, sin_ref, q_ref, k_ref, v_ref):
    cos = cos_ref[...]
    sin = sin_ref[...]
    qn = _rms(cq_ref[...].astype(F32), qn_ref[...]).astype(BF16)
    a = jnp.dot(qn, wqa_ref[...], preferred_element_type=F32)
    bsw = jnp.dot(qn, wqb_ref[...], preferred_element_type=F32)
    kvn = _rms(ckv_ref[...].astype(F32), kvn_ref[...]).astype(BF16)
    kn = jnp.dot(kvn, wk_ref[...], preferred_element_type=F32)
    v_ref[...] = jnp.dot(kvn, wv_ref[...], preferred_element_type=F32).astype(BF16)
    sm = sm_ref[...]
    kpe = (sm[:, 0:LANES] * cos + sm[:, LANES:2 * LANES] * sin).astype(BF16)
    for h in range(HEADS):
        lo = h * MLA_QK
        q_ref[:, lo:lo + LANES] = a[:, lo:lo + LANES].astype(BF16)
        q_ref[:, lo + LANES:lo + 2 * LANES] = (
            a[:, lo + LANES:lo + 2 * LANES] * cos + bsw[:, h * LANES:(h + 1) * LANES] * sin
        ).astype(BF16)
        k_ref[:, lo:lo + LANES] = kn[:, h * LANES:(h + 1) * LANES].astype(BF16)
        k_ref[:, lo + LANES:lo + 2 * LANES] = kpe


def _mla_prep(p_main, p_small, qn, kvn, wqa, wqb, wk, wv, cos_t, sin_t, S, tm=512):
    T = p_main.shape[0]
    nS = S // tm
    const = lambda i: (0, 0)
    return pl.pallas_call(
        _mla_prep_kernel,
        grid=(T // tm,),
        in_specs=[
            pl.BlockSpec((tm, MLA_RANK), lambda i: (i, C_CQ // MLA_RANK)),
            pl.BlockSpec((tm, MLA_RANK), lambda i: (i, C_CKV // MLA_RANK)),
            pl.BlockSpec((tm, N_SMALL), lambda i: (i, 0)),
            pl.BlockSpec((1, MLA_RANK), const),
            pl.BlockSpec((1, MLA_RANK), const),
            pl.BlockSpec((MLA_RANK, HEADS * MLA_QK), const),
            pl.BlockSpec((MLA_RANK, HEADS * LANES), const),
            pl.BlockSpec((MLA_RANK, BRANCH_W), const),
            pl.BlockSpec((MLA_RANK, BRANCH_W), const),
            pl.BlockSpec((tm, LANES), lambda i: (i % nS, 0)),
            pl.BlockSpec((tm, LANES), lambda i: (i % nS, 0)),
        ],
        out_specs=[
            pl.BlockSpec((tm, HEADS * MLA_QK), lambda i: (i, 0)),
            pl.BlockSpec((tm, HEADS * MLA_QK), lambda i: (i, 0)),
            pl.BlockSpec((tm, BRANCH_W), lambda i: (i, 0)),
        ],
        out_shape=[
            jax.ShapeDtypeStruct((T, HEADS * MLA_QK), BF16),
            jax.ShapeDtypeStruct((T, HEADS * MLA_QK), BF16),
            jax.ShapeDtypeStruct((T, BRANCH_W), BF16),
        ],
        compiler_params=_cparams(("parallel",)),
        name="mla_prep",
    )(p_main, p_main, p_small, qn, kvn, wqa, wqb, wk, wv, cos_t, sin_t)


def _gdn_kernel(q_ref, k_ref, v_ref, z_ref, gt_ref, cwq_ref, cwk_ref, cwv_ref, on_ref,
                mk_ref, o_ref, qn_s, kn_s, vn_s, g_s, b_s, p_s, n_s, qp_s, oo_s, st_s, cd_s, xp_s):
    h = pl.program_id(1)
    S = q_ref.shape[1]
    C = GDN_C
    lane = lax.broadcasted_iota(jnp.int32, (S, LANES), 1)

    xp_s[0:8, :] = jnp.zeros((8, LANES), F32)

    def conv_silu(x_ref, cw_ref):
        w = cw_ref[...] * 0.5
        xp_s[8:, :] = x_ref[0].astype(F32)
        hy = None
        for sft in range(GDN_CONV):
            t = xp_s[8 - sft:8 - sft + S, :] * w[GDN_CONV - 1 - sft:GDN_CONV - sft]
            hy = t if hy is None else hy + t
        return hy + hy * jnp.tanh(hy)

    def l2n(x):
        return x * lax.rsqrt(jnp.sum(x * x, axis=-1, keepdims=True) + EPS)

    qn_s[...] = l2n(conv_silu(q_ref, cwq_ref)) * (HEAD_DIM ** -0.5)
    kn_s[...] = l2n(conv_silu(k_ref, cwk_ref))
    vn_s[...] = conv_silu(v_ref, cwv_ref)

    gates = gt_ref[0]
    pick = lambda col: jnp.broadcast_to(
        jnp.sum(jnp.where(lane == col + h, gates, 0.0), axis=-1, keepdims=True), (S, LANES))
    b_s[...] = pick(S_GB)
    g_s[...] = pick(S_GA)

    tri = mk_ref[0]
    eye = mk_ref[1]
    incl = tri > 0.5
    strict = tri - eye

    nt = (((1,), (1,)), ((), ()))
    dot = functools.partial(jnp.dot, preferred_element_type=F32)

    def local(n, _):
        G = range(GDN_GROUP)
        sls = [pl.ds(pl.multiple_of((n * GDN_GROUP + g) * C, C), C) for g in G]
        q = [qn_s[sl, :] for sl in sls]
        k = [kn_s[sl, :] for sl in sls]
        beta = [b_s[sl, :] for sl in sls]
        gc = [jnp.dot(tri, g_s[sl, :], precision=HIGHEST, preferred_element_type=F32) for sl in sls]
        decay = [jnp.exp(jnp.where(incl, x - x.T, -jnp.inf)) for x in gc]
        kb = [k[g] * beta[g] for g in G]
        kbf = [x.astype(BF16) for x in k]
        a = [lax.dot_general(kb[g].astype(BF16), kbf[g], nt, preferred_element_type=F32)
             * decay[g] * strict for g in G]
        intra = [(lax.dot_general(q[g].astype(BF16), kbf[g], nt, preferred_element_type=F32)
                  * decay[g]).astype(BF16) for g in G]
        t = [eye - a[g] * mk_ref[2] for g in G]
        for lvl in range(1, int(math.log2(C))):
            tb = [x.astype(BF16) for x in t]
            tc = [dot(tb[g], (a[g] * mk_ref[2 + lvl]).astype(BF16)).astype(BF16) for g in G]
            t = [t[g] - dot(tc[g], tb[g]) for g in G]
        tb = [x.astype(BF16) for x in t]
        eg = [jnp.exp(x) for x in gc]
        ub = [dot(tb[g], (vn_s[sls[g], :] * beta[g]).astype(BF16)).astype(BF16) for g in G]
        wb = [dot(tb[g], (kb[g] * eg[g]).astype(BF16)).astype(BF16) for g in G]
        glast = [x[C - 1:C, :] for x in gc]
        kdt = [(k[g] * jnp.exp(glast[g] - gc[g])).T.astype(BF16) for g in G]
        for g in G:
            p_s[sls[g], :] = dot(kdt[g], wb[g]).astype(BF16)
        for g in G:
            n_s[sls[g], :] = dot(kdt[g], ub[g])
        for g in G:
            qp_s[sls[g], :] = (q[g] * eg[g] - dot(intra[g], wb[g])).astype(BF16)
        for g in G:
            oo_s[sls[g], :] = dot(intra[g], ub[g])
            cd_s[n * GDN_GROUP + g] = jnp.broadcast_to(jnp.exp(glast[g]), (8, LANES))
        return 0

    lax.fori_loop(0, S // C // GDN_GROUP, local, 0)

    state = jnp.zeros((HEAD_DIM, HEAD_DIM), F32)
    for n in range(S // C):
        sb = state.astype(BF16)
        st_s[n * C:(n + 1) * C, :] = sb
        state = (state * cd_s[n][0:1, :] + n_s[n * C:(n + 1) * C, :]
                 - dot(p_s[n * C:(n + 1) * C, :], sb))

    def emit(n, _):
        G = range(GDN_GROUP)
        sls = [pl.ds(pl.multiple_of((n * GDN_GROUP + g) * C, C), C) for g in G]
        outs = [dot(qp_s[sl, :], st_s[sl, :]) + oo_s[sl, :] for sl in sls]
        for sl, out in zip(sls, outs):
            z = z_ref[0, sl, :].astype(F32)
            o_ref[0, sl, :] = (_rms(out, on_ref[...]) * (z * _sigmoid(z))).astype(BF16)
        return 0

    lax.fori_loop(0, S // C // GDN_GROUP, emit, 0)


def _gdn_masks():
    C = GDN_C
    i = np.arange(C)[:, None]
    j = np.arange(C)[None, :]
    mats = [(i >= j), (i == j)]
    s = 1
    while s < C:
        mats.append(((i // (2 * s)) == (j // (2 * s))) & ((i // s) % 2 == 1) & ((j // s) % 2 == 0))
        s *= 2
    return jnp.asarray(np.stack(mats).astype(np.float32))


def _gdn(p_main3, gates, conv_w, out_norm, masks):
    B, S, _ = p_main3.shape
    assert S % (GDN_C * GDN_GROUP) == 0, S
    nm = masks.shape[0]
    cb = lambda base: (lambda b, h: (b, 0, base // LANES + h))
    cw = lambda base: (lambda b, h: (0, base // LANES + h))
    seq = lambda dt=F32: pltpu.VMEM((S, LANES), dt)
    return pl.pallas_call(
        _gdn_kernel,
        grid=(B, HEADS),
        in_specs=[
            pl.BlockSpec((1, S, LANES), cb(C_GQ)),
            pl.BlockSpec((1, S, LANES), cb(C_GK)),
            pl.BlockSpec((1, S, LANES), cb(C_GV)),
            pl.BlockSpec((1, S, LANES), cb(C_GZ)),
            pl.BlockSpec((1, S, LANES), lambda b, h: (b, 0, 0)),
            pl.BlockSpec((GDN_CONV, LANES), cw(0)),
            pl.BlockSpec((GDN_CONV, LANES), cw(BRANCH_W)),
            pl.BlockSpec((GDN_CONV, LANES), cw(2 * BRANCH_W)),
            pl.BlockSpec((1, LANES), lambda b, h: (0, 0)),
            pl.BlockSpec((nm, GDN_C, GDN_C), lambda b, h: (0, 0, 0)),
        ],
        out_specs=pl.BlockSpec((1, S, LANES), lambda b, h: (b, 0, h)),
        out_shape=jax.ShapeDtypeStruct((B, S, BRANCH_W), BF16),
        scratch_shapes=[seq() for _ in range(5)]
        + [seq(BF16), seq(), seq(BF16), seq(), seq(BF16)]
        + [pltpu.VMEM((S // GDN_C, 8, LANES), F32), pltpu.VMEM((S + 8, LANES), F32)],
        compiler_params=_cparams(("parallel", "arbitrary")),
        name="gdn",
    )(p_main3, p_main3, p_main3, p_main3, gates, conv_w, conv_w, conv_w, out_norm, masks)


def _merge_kernel(of_ref, om_ref, og_ref, wb_ref, g0_ref, g1_ref, g2_ref, o_ref):
    acc = None
    for o, g, br in ((of_ref, g0_ref, 0), (om_ref, g1_ref, 1), (og_ref, g2_ref, 2)):
        y = _sigmoid(g[...].astype(F32)) * jnp.dot(o[...], wb_ref[br], preferred_element_type=F32)
        acc = y if acc is None else acc + y
    o_ref[...] = acc.astype(BF16)


def _merge(o_fox, o_mla, o_gdn, w_branch, p_main, tm=1024, tn=512):
    T = o_fox.shape[0]
    osp = pl.BlockSpec((tm, BRANCH_W), lambda i, j: (i, 0))
    gsp = lambda br: pl.BlockSpec((tm, tn), lambda i, j: (i, (C_GATE + br * D_MODEL) // tn + j))
    return pl.pallas_call(
        _merge_kernel,
        grid=(T // tm, D_MODEL // tn),
        in_specs=[osp, osp, osp,
                  pl.BlockSpec((3, BRANCH_W, tn), lambda i, j: (0, 0, j)),
                  gsp(0), gsp(1), gsp(2)],
        out_specs=pl.BlockSpec((tm, tn), lambda i, j: (i, j)),
        out_shape=jax.ShapeDtypeStruct((T, D_MODEL), BF16),
        compiler_params=_cparams(("parallel", "arbitrary")),
        name="merge",
    )(o_fox, o_mla, o_gdn, w_branch, p_main, p_main, p_main)


def _outproj_kernel(m_ref, w_ref, x_ref, o_ref):
    o_ref[...] = x_ref[...] + jnp.dot(m_ref[...], w_ref[...], preferred_element_type=F32)


def _outproj(merged, w_out, x2, tm=1024, tn=1024):
    T = merged.shape[0]
    return pl.pallas_call(
        _outproj_kernel,
        grid=(T // tm, D_MODEL // tn),
        in_specs=[
            pl.BlockSpec((tm, D_MODEL), lambda i, j: (i, 0)),
            pl.BlockSpec((D_MODEL, tn), lambda i, j: (0, j)),
            pl.BlockSpec((tm, tn), lambda i, j: (i, j)),
        ],
        out_specs=pl.BlockSpec((tm, tn), lambda i, j: (i, j)),
        out_shape=jax.ShapeDtypeStruct((T, D_MODEL), F32),
        compiler_params=_cparams(("parallel", "arbitrary")),
        name="outproj",
    )(merged, w_out, x2)


def _ffn_kernel(x_ref, g_ref, wu_ref, wd_ref, fg_ref, o_ref, h_ref, *, final):
    f = pl.program_id(1)

    @pl.when(f == 0)
    def _():
        x = x_ref[...]
        h_ref[...] = _rms(x, g_ref[...]).astype(BF16)
        o_ref[...] = x

    hid = jnp.maximum(jnp.dot(h_ref[...], wu_ref[...], preferred_element_type=F32), 0.0)
    o_ref[...] += jnp.dot((hid * hid).astype(BF16), wd_ref[...], preferred_element_type=F32)

    if final:
        @pl.when(f == pl.num_programs(1) - 1)
        def _():
            o_ref[...] = _rms(o_ref[...], fg_ref[...])


def _ffn(x2, gain, w_up, w_down, final_gain, final, tm=1024, tf=512):
    T = x2.shape[0]
    return pl.pallas_call(
        functools.partial(_ffn_kernel, final=final),
        grid=(T // tm, D_FF // tf),
        in_specs=[
            pl.BlockSpec((tm, D_MODEL), lambda i, f: (i, 0), pipeline_mode=pl.Buffered(1)),
            pl.BlockSpec((1, D_MODEL), lambda i, f: (0, 0)),
            pl.BlockSpec((D_MODEL, tf), lambda i, f: (0, f)),
            pl.BlockSpec((tf, D_MODEL), lambda i, f: (f, 0)),
            pl.BlockSpec((1, D_MODEL), lambda i, f: (0, 0)),
        ],
        out_specs=pl.BlockSpec((tm, D_MODEL), lambda i, f: (i, 0)),
        out_shape=jax.ShapeDtypeStruct((T, D_MODEL), F32),
        scratch_shapes=[pltpu.VMEM((tm, D_MODEL), BF16)],
        compiler_params=_cparams(("parallel", "arbitrary")),
        name="ffn",
    )(x2, gain, w_up, w_down, final_gain)


def _rope_tables(S):
    inv_freq = ROPE_THETA ** (-jnp.arange(0, MLA_ROPE, 2, dtype=F32) / MLA_ROPE)
    ang = jnp.arange(S, dtype=F32)[:, None] * inv_freq[None, :]
    cos, sin = jnp.cos(ang), jnp.sin(ang)
    zeros = jnp.zeros((S, LANES - MLA_ROPE), F32)
    return (jnp.concatenate([cos, cos, zeros], axis=1),
            jnp.concatenate([-sin, sin, zeros], axis=1))


def _swap_halves(w):
    half = MLA_ROPE // 2
    return jnp.concatenate([w[..., half:], w[..., :half]], axis=-1)


def _split_w_in(w):
    sizes = (BRANCH_W, BRANCH_W, BRANCH_W, HEADS, MLA_RANK, MLA_RANK, MLA_ROPE,
             3 * BRANCH_W, BRANCH_W, HEADS, HEADS, 3 * D_MODEL)
    offs = np.cumsum((0,) + sizes)
    seg = lambda n: w[:, offs[n]:offs[n + 1]]
    fq, fk, fv, ff, cq, ckv, kpe, gqkv, gz, gb, ga, gate = (seg(n) for n in range(12))
    main = jnp.concatenate([fq, fk, fv, cq, ckv, gqkv, gz, gate], axis=1).astype(BF16)
    z = lambda n: jnp.zeros((D_MODEL, n), F32)
    small = jnp.concatenate(
        [kpe, z(LANES - MLA_ROPE), _swap_halves(kpe), z(LANES - MLA_ROPE),
         ff, gb, ga, z(LANES - 3 * HEADS)], axis=1).astype(BF16)
    return main, small


def _split_mla(w_uq, w_ukv):
    nope, pe = w_uq[..., :HEAD_DIM], w_uq[..., HEAD_DIM:]
    zpad = jnp.zeros((MLA_RANK, HEADS, LANES - MLA_ROPE), F32)
    wqa = jnp.concatenate([nope, pe, zpad], axis=-1).reshape(MLA_RANK, HEADS * MLA_QK)
    wqb = jnp.concatenate([_swap_halves(pe), zpad], axis=-1).reshape(MLA_RANK, HEADS * LANES)
    wk = w_ukv[..., :HEAD_DIM].reshape(MLA_RANK, BRANCH_W)
    wv = w_ukv[..., HEAD_DIM:].reshape(MLA_RANK, BRANCH_W)
    return tuple(a.astype(BF16) for a in (wqa, wqb, wk, wv))


@jax.jit
def _forward(x, attn_norm, w_in, fox_fgate_bias, mla_q_norm, mla_kv_norm, w_mla_uq, w_mla_ukv,
             gdn_conv, gdn_a_log, gdn_dt_bias, gdn_out_norm, w_branch, w_out, mlp_norm,
             w_up, w_down, final_norm):
    B, S, D = x.shape
    T = B * S
    depth = w_in.shape[0]
    cos_t, sin_t = _rope_tables(S)
    masks = _gdn_masks()
    row = lambda v: v.reshape(1, -1).astype(F32)
    x2 = x.reshape(T, D)
    for l in range(depth):
        w_main, w_small = _split_w_in(w_in[l])
        p_main, p_small = _proj_in(x2, row(attn_norm[l]), w_main, w_small)
        pm3 = p_main.reshape(B, S, N_MAIN)
        ps3 = p_small.reshape(B, S, N_SMALL)

        bias_row = (jnp.zeros((1, LANES), F32).at[0, S_FF:S_FF + HEADS].set(fox_fgate_bias[l])
                    .at[0, S_GA:S_GA + HEADS].set(gdn_dt_bias[l]))
        alog_row = jnp.zeros((1, LANES), F32).at[0, S_GA:S_GA + HEADS].set(gdn_a_log[l])
        cum_c, cum_r, gdn_gates = _gates(ps3, bias_row, alog_row)
        cum = (cum_c, cum_r)
        fv_t = jnp.swapaxes(pm3[:, :, C_FV:C_FV + BRANCH_W], 1, 2)
        o_fox = _attention(pm3, pm3, fv_t, C_FQ // BRANCH_W, C_FK // BRANCH_W,
                           HEAD_DIM, HEAD_DIM ** -0.5, cum=cum)

        wqa, wqb, wk, wv = _split_mla(w_mla_uq[l], w_mla_ukv[l])
        qf, kf, vf = _mla_prep(p_main, p_small, row(mla_q_norm[l]), row(mla_kv_norm[l]),
                               wqa, wqb, wk, wv, cos_t, sin_t, S)
        shp = lambda a: a.reshape(B, S, a.shape[-1])
        o_mla = _attention(shp(qf), shp(kf), jnp.swapaxes(shp(vf), 1, 2), 0, 0, MLA_QK,
                           (HEAD_DIM + MLA_ROPE) ** -0.5)

        o_gdn = _gdn(pm3, gdn_gates, gdn_conv[l], row(gdn_out_norm[l]), masks)

        merged = _merge(o_fox.reshape(T, BRANCH_W), o_mla.reshape(T, BRANCH_W),
                        o_gdn.reshape(T, BRANCH_W), w_branch[l].astype(BF16), p_main)
        x2 = _outproj(merged, w_out[l].astype(BF16), x2)
        x2 = _ffn(x2, row(mlp_norm[l]), w_up[l].astype(BF16), w_down[l].astype(BF16),
                  row(final_norm), final=(l == depth - 1))
    return x2.reshape(B, S, D)


def kernel(x, attn_norm, w_in, fox_fgate_bias, mla_q_norm, mla_kv_norm, w_mla_uq, w_mla_ukv,
           gdn_conv, gdn_a_log, gdn_dt_bias, gdn_out_norm, w_branch, w_out, mlp_norm,
           w_up, w_down, final_norm):
    return _forward(x, attn_norm, w_in, fox_fgate_bias, mla_q_norm, mla_kv_norm, w_mla_uq,
                    w_mla_ukv, gdn_conv, gdn_a_log, gdn_dt_bias, gdn_out_norm, w_branch, w_out,
                    mlp_norm, w_up, w_down, final_norm)
```

```python
import functools
import math

import numpy as np
import jax
import jax.numpy as jnp
from jax import lax
from jax.experimental import pallas as pl
from jax.experimental.pallas import tpu as pltpu

F32 = jnp.float32
BF16 = jnp.bfloat16
HIGHEST = lax.Precision.HIGHEST

D_MODEL = 2048
HEAD_DIM = 128
HEADS = 8
BRANCH_W = HEADS * HEAD_DIM
MLA_RANK = 512
MLA_ROPE = 64
MLA_QK = 256
ROPE_THETA = 10000.0
GDN_CONV = 4
D_FF = 4 * D_MODEL
EPS = 1e-6
LOG2E = math.log2(math.e)

LANES = 128
VMEM_LIMIT = 52 * 1024 * 1024

C_FQ, C_FK = 0, 1024
C_CQ, C_CKV = 2048, 2560
C_GQ, C_GK, C_GV, C_GZ = 3072, 4096, 5120, 6144
C_GATE = 7168
N_MAIN = C_GATE + 3 * D_MODEL
N_SMALL = 3 * LANES
S_FF, S_GB, S_GA = 0, 8, 16

GDN_C = 128
ATTN_GROUP = 8
GDN_GROUP = 8


def _cparams(sem):
    return pltpu.CompilerParams(dimension_semantics=sem, vmem_limit_bytes=VMEM_LIMIT)


def _sigmoid(x):
    return 0.5 * jnp.tanh(0.5 * x) + 0.5


def _softplus(x):
    return jnp.maximum(x, 0.0) + jnp.log(1.0 + jnp.exp(-jnp.abs(x)))


def _rms(x, gain):
    return x * lax.rsqrt(jnp.mean(x * x, axis=-1, keepdims=True) + EPS) * gain


def _proj_kernel(x_ref, g_ref, w_ref, ws_ref, wvt_ref, o_ref, os_ref, vt_ref, u_ref):
    @pl.when(pl.program_id(1) == 0)
    def _():
        u = _rms(x_ref[...], g_ref[...]).astype(BF16)
        u_ref[...] = u
        os_ref[...] = jnp.dot(u, ws_ref[...], preferred_element_type=F32)
        vt_ref[0] = lax.dot_general(wvt_ref[...], u, (((1,), (1,)), ((), ())),
                                    preferred_element_type=F32).astype(BF16)

    o_ref[...] = jnp.dot(u_ref[...], w_ref[...], preferred_element_type=F32).astype(BF16)


def _proj_in(x2, gain, w_main, w_small, w_fvt, S, tm=1024, tn=1024):
    T = x2.shape[0]
    nS = S // tm
    return pl.pallas_call(
        _proj_kernel,
        grid=(T // tm, N_MAIN // tn),
        in_specs=[
            pl.BlockSpec((tm, D_MODEL), lambda i, j: (i, 0), pipeline_mode=pl.Buffered(1)),
            pl.BlockSpec((1, D_MODEL), lambda i, j: (0, 0)),
            pl.BlockSpec((D_MODEL, tn), lambda i, j: (0, j)),
            pl.BlockSpec((D_MODEL, N_SMALL), lambda i, j: (0, 0)),
            pl.BlockSpec((BRANCH_W, D_MODEL), lambda i, j: (0, 0)),
        ],
        out_specs=[
            pl.BlockSpec((tm, tn), lambda i, j: (i, j)),
            pl.BlockSpec((tm, N_SMALL), lambda i, j: (i, 0)),
            pl.BlockSpec((1, BRANCH_W, tm), lambda i, j: (i // nS, 0, i % nS)),
        ],
        out_shape=[
            jax.ShapeDtypeStruct((T, N_MAIN), BF16),
            jax.ShapeDtypeStruct((T, N_SMALL), F32),
            jax.ShapeDtypeStruct((T // S, BRANCH_W, S), BF16),
        ],
        scratch_shapes=[pltpu.VMEM((tm, D_MODEL), BF16)],
        compiler_params=_cparams(("parallel", "arbitrary")),
        name="proj_in",
    )(x2, gain, w_main, w_small, w_fvt)


def _cumsum_kernel(s_ref, b_ref, al_ref, c_ref, ct_ref, gt_ref):
    S = s_ref.shape[1]
    x = s_ref[0] + b_ref[...]
    lane = lax.broadcasted_iota(jnp.int32, (S, LANES), 1)
    gt_ref[0] = jnp.where(lane < S_GA, _sigmoid(x), -jnp.exp(al_ref[...]) * _softplus(x))
    logf = -_softplus(-x) * LOG2E
    r = lax.broadcasted_iota(jnp.int32, (LANES, LANES), 0)
    c = lax.broadcasted_iota(jnp.int32, (LANES, LANES), 1)
    tri = (r >= c).astype(F32)
    carry = jnp.zeros((1, LANES), F32)
    for blk in range(S // LANES):
        seg = logf[blk * LANES:(blk + 1) * LANES]
        cs = jnp.dot(tri, seg, precision=HIGHEST, preferred_element_type=F32) + carry
        c_ref[0, blk * LANES:(blk + 1) * LANES, :] = cs
        carry = cs[LANES - 1:LANES, :]
    ct_ref[0] = c_ref[0].T[:HEADS]


def _gates(small3, bias_row, alog_row):
    B, S, _ = small3.shape
    return pl.pallas_call(
        _cumsum_kernel,
        grid=(B,),
        in_specs=[
            pl.BlockSpec((1, S, LANES), lambda b: (b, 0, 2)),
            pl.BlockSpec((1, LANES), lambda b: (0, 0)),
            pl.BlockSpec((1, LANES), lambda b: (0, 0)),
        ],
        out_specs=[
            pl.BlockSpec((1, S, LANES), lambda b: (b, 0, 0)),
            pl.BlockSpec((1, HEADS, S), lambda b: (b, 0, 0)),
            pl.BlockSpec((1, S, LANES), lambda b: (b, 0, 0)),
        ],
        out_shape=[
            jax.ShapeDtypeStruct((B, S, LANES), F32),
            jax.ShapeDtypeStruct((B, HEADS, S), F32),
            jax.ShapeDtypeStruct((B, S, LANES), F32),
        ],
        compiler_params=_cparams(("parallel",)),
        name="gates",
    )(small3, bias_row, alog_row)


def _attn_kernel(*refs, dk, scale, decay, tq):
    if decay:
        q_ref, k_ref, vt_ref, cq_ref, ck_ref, o_ref = refs
    else:
        q_ref, k_ref, vt_ref, o_ref = refs
    i = pl.program_id(1)
    krow = lax.broadcasted_iota(jnp.int32, (tq, tq), 0)
    qcol = lax.broadcasted_iota(jnp.int32, (tq, tq), 1)
    causal = krow <= qcol
    c2 = scale * LOG2E

    def group_step(hs, qs, cqs, j, carry, masked):
        off = pl.multiple_of(j * tq, tq)
        ts = []
        for h, q in zip(hs, qs):
            k = k_ref[0, pl.ds(off, tq), h * dk:(h + 1) * dk]
            t = lax.dot_general(k, q, (((1,), (1,)), ((), ())), preferred_element_type=F32) * c2
            if decay:
                t = t - ck_ref[0, pl.ds(off, tq), h:h + 1]
            if masked:
                t = jnp.where(causal, t, -jnp.inf)
            ts.append(t)
        ps = []
        for t, cq, (m, l, acc) in zip(ts, cqs, carry):
            mt = jnp.max(t, axis=0, keepdims=True)
            m_new = jnp.maximum(m, mt + cq if decay else mt)
            p = jnp.exp2(t - (m_new - cq if decay else m_new))
            alpha = jnp.exp2(m - m_new)
            ps.append((m_new, alpha, alpha * l + jnp.sum(p, axis=0, keepdims=True), p.astype(BF16)))
        out = []
        for h, (m_new, alpha, l, p), (_, _, acc) in zip(hs, ps, carry):
            vt = vt_ref[0, h * HEAD_DIM:(h + 1) * HEAD_DIM, pl.ds(off, tq)]
            out.append((m_new, l, alpha * acc + jnp.dot(vt, p, preferred_element_type=F32)))
        return tuple(out)

    for h0 in range(0, HEADS, ATTN_GROUP):
        hs = tuple(range(h0, h0 + ATTN_GROUP))
        qs = [q_ref[0, :, h * dk:(h + 1) * dk] for h in hs]
        cqs = [cq_ref[0, h:h + 1, :] if decay else None for h in hs]

        def step(j, carry, masked, hs=hs, qs=qs, cqs=cqs):
            return group_step(hs, qs, cqs, j, carry, masked)

        init = tuple((jnp.full((1, tq), -jnp.inf, F32), jnp.zeros((1, tq), F32),
                      jnp.zeros((HEAD_DIM, tq), F32)) for _ in hs)
        carry = lax.fori_loop(0, i, functools.partial(step, masked=False), init)
        for h, (_, l, acc) in zip(hs, step(i, carry, masked=True)):
            o_ref[0, :, h * HEAD_DIM:(h + 1) * HEAD_DIM] = (acc / l).T.astype(BF16)


def _attention(q_arr, k_arr, vt_arr, q_blk, k_blk, dk, scale, cum=None, tq=512):
    B, S, _ = q_arr.shape
    decay = cum is not None
    in_specs = [
        pl.BlockSpec((1, tq, HEADS * dk), lambda b, i: (b, i, q_blk)),
        pl.BlockSpec((1, S, HEADS * dk), lambda b, i: (b, 0, k_blk)),
        pl.BlockSpec((1, BRANCH_W, S), lambda b, i: (b, 0, 0)),
    ]
    args = [q_arr, k_arr, vt_arr]
    if decay:
        in_specs += [
            pl.BlockSpec((1, HEADS, tq), lambda b, i: (b, 0, i)),
            pl.BlockSpec((1, S, LANES), lambda b, i: (b, 0, 0)),
        ]
        args += [cum[1], cum[0]]
    return pl.pallas_call(
        functools.partial(_attn_kernel, dk=dk, scale=scale, decay=decay, tq=tq),
        grid=(B, S // tq),
        in_specs=in_specs,
        out_specs=pl.BlockSpec((1, tq, BRANCH_W), lambda b, i: (b, i, 0)),
        out_shape=jax.ShapeDtypeStruct((B, S, BRANCH_W), BF16),
        compiler_params=_cparams(("parallel", "arbitrary")),
        name="fox_attn" if decay else "mla_attn",
    )(*args)


def _mla_prep_kernel(cq_ref, ckv_ref, sm_ref, qn_ref, kvn_ref, wqa_ref, wqb_ref, wk_ref, wv_ref,
                     cos_ref, sin_ref, q_ref, k_ref, v_ref):
    cos = cos_ref[...]
    sin = sin_ref[...]
    qn = _rms(cq_ref[...].astype(F32), qn_ref[...]).astype(BF16)
    a = jnp.dot(qn, wqa_ref[...], preferred_element_type=F32)
    bsw = jnp.dot(qn, wqb_ref[...], preferred_element_type=F32)
    kvn = _rms(ckv_ref[...].astype(F32), kvn_ref[...]).astype(BF16)
    kn = jnp.dot(kvn, wk_ref[...], preferred_element_type=F32)
    v_ref[0] = lax.dot_general(wv_ref[...], kvn, (((1,), (1,)), ((), ())),
                               preferred_element_type=F32).astype(BF16)
    sm = sm_ref[...]
    kpe = (sm[:, 0:LANES] * cos + sm[:, LANES:2 * LANES] * sin).astype(BF16)
    for h in range(HEADS):
        lo = h * MLA_QK
        q_ref[:, lo:lo + LANES] = a[:, lo:lo + LANES].astype(BF16)
        q_ref[:, lo + LANES:lo + 2 * LANES] = (
            a[:, lo + LANES:lo + 2 * LANES] * cos + bsw[:, h * LANES:(h + 1) * LANES] * sin
        ).astype(BF16)
        k_ref[:, lo:lo + LANES] = kn[:, h * LANES:(h + 1) * LANES].astype(BF16)
        k_ref[:, lo + LANES:lo + 2 * LANES] = kpe


def _mla_prep(p_main, p_small, qn, kvn, wqa, wqb, wk, wv, cos_t, sin_t, S, tm=512):
    T = p_main.shape[0]
    nS = S // tm
    const = lambda i: (0, 0)
    return pl.pallas_call(
        _mla_prep_kernel,
        grid=(T // tm,),
        in_specs=[
            pl.BlockSpec((tm, MLA_RANK), lambda i: (i, C_CQ // MLA_RANK)),
            pl.BlockSpec((tm, MLA_RANK), lambda i: (i, C_CKV // MLA_RANK)),
            pl.BlockSpec((tm, N_SMALL), lambda i: (i, 0)),
            pl.BlockSpec((1, MLA_RANK), const),
            pl.BlockSpec((1, MLA_RANK), const),
            pl.BlockSpec((MLA_RANK, HEADS * MLA_QK), const),
            pl.BlockSpec((MLA_RANK, HEADS * LANES), const),
            pl.BlockSpec((MLA_RANK, BRANCH_W), const),
            pl.BlockSpec((BRANCH_W, MLA_RANK), const),
            pl.BlockSpec((tm, LANES), lambda i: (i % nS, 0)),
            pl.BlockSpec((tm, LANES), lambda i: (i % nS, 0)),
        ],
        out_specs=[
            pl.BlockSpec((tm, HEADS * MLA_QK), lambda i: (i, 0)),
            pl.BlockSpec((tm, HEADS * MLA_QK), lambda i: (i, 0)),
            pl.BlockSpec((1, BRANCH_W, tm), lambda i: (i // nS, 0, i % nS)),
        ],
        out_shape=[
            jax.ShapeDtypeStruct((T, HEADS * MLA_QK), BF16),
            jax.ShapeDtypeStruct((T, HEADS * MLA_QK), BF16),
            jax.ShapeDtypeStruct((T // S, BRANCH_W, S), BF16),
        ],
        compiler_params=_cparams(("parallel",)),
        name="mla_prep",
    )(p_main, p_main, p_small, qn, kvn, wqa, wqb, wk, wv, cos_t, sin_t)


def _gdn_kernel(q_ref, k_ref, v_ref, z_ref, gt_ref, cwq_ref, cwk_ref, cwv_ref, on_ref,
                mk_ref, o_ref, qn_s, kn_s, vn_s, g_s, b_s, p_s, n_s, qp_s, oo_s, st_s, cd_s, xp_s):
    h = pl.program_id(1)
    S = q_ref.shape[1]
    C = GDN_C
    lane = lax.broadcasted_iota(jnp.int32, (S, LANES), 1)

    xp_s[0:8, :] = jnp.zeros((8, LANES), F32)

    def conv_silu(x_ref, cw_ref):
        w = cw_ref[...] * 0.5
        xp_s[8:, :] = x_ref[0].astype(F32)
        hy = None
        for sft in range(GDN_CONV):
            t = xp_s[8 - sft:8 - sft + S, :] * w[GDN_CONV - 1 - sft:GDN_CONV - sft]
            hy = t if hy is None else hy + t
        return hy + hy * jnp.tanh(hy)

    def l2n(x):
        return x * lax.rsqrt(jnp.sum(x * x, axis=-1, keepdims=True) + EPS)

    qn_s[...] = l2n(conv_silu(q_ref, cwq_ref)) * (HEAD_DIM ** -0.5)
    kn_s[...] = l2n(conv_silu(k_ref, cwk_ref))
    vn_s[...] = conv_silu(v_ref, cwv_ref)

    gates = gt_ref[0]
    pick = lambda col: jnp.broadcast_to(
        jnp.sum(jnp.where(lane == col + h, gates, 0.0), axis=-1, keepdims=True), (S, LANES))
    b_s[...] = pick(S_GB)
    g_s[...] = pick(S_GA)

    tri = mk_ref[0]
    eye = mk_ref[1]
    incl = tri > 0.5
    strict = tri - eye

    nt = (((1,), (1,)), ((), ()))
    dot = functools.partial(jnp.dot, preferred_element_type=F32)

    def local(n, _):
        G = range(GDN_GROUP)
        sls = [pl.ds(pl.multiple_of((n * GDN_GROUP + g) * C, C), C) for g in G]
        q = [qn_s[sl, :] for sl in sls]
        k = [kn_s[sl, :] for sl in sls]
        beta = [b_s[sl, :] for sl in sls]
        gc = [jnp.dot(tri, g_s[sl, :], precision=HIGHEST, preferred_element_type=F32) for sl in sls]
        decay = [jnp.exp(jnp.where(incl, x - x.T, -jnp.inf)) for x in gc]
        kb = [k[g] * beta[g] for g in G]
        kbf = [x.astype(BF16) for x in k]
        a = [lax.dot_general(kb[g].astype(BF16), kbf[g], nt, preferred_element_type=F32)
             * decay[g] * strict for g in G]
        intra = [(lax.dot_general(q[g].astype(BF16), kbf[g], nt, preferred_element_type=F32)
                  * decay[g]).astype(BF16) for g in G]
        t = [eye - a[g] * mk_ref[2] for g in G]
        for lvl in range(1, int(math.log2(C))):
            tb = [x.astype(BF16) for x in t]
            tc = [dot(tb[g], (a[g] * mk_ref[2 + lvl]).astype(BF16)).astype(BF16) for g in G]
            t = [t[g] - dot(tc[g], tb[g]) for g in G]
        tb = [x.astype(BF16) for x in t]
        eg = [jnp.exp(x) for x in gc]
        ub = [dot(tb[g], (vn_s[sls[g], :] * beta[g]).astype(BF16)).astype(BF16) for g in G]
        wb = [dot(tb[g], (kb[g] * eg[g]).astype(BF16)).astype(BF16) for g in G]
        glast = [x[C - 1:C, :] for x in gc]
        kdt = [(k[g] * jnp.exp(glast[g] - gc[g])).T.astype(BF16) for g in G]
        for g in G:
            p_s[sls[g], :] = dot(kdt[g], wb[g]).astype(BF16)
        for g in G:
            n_s[sls[g], :] = dot(kdt[g], ub[g])
        for g in G:
            qp_s[sls[g], :] = (q[g] * eg[g] - dot(intra[g], wb[g])).astype(BF16)
        for g in G:
            oo_s[sls[g], :] = dot(intra[g], ub[g])
            cd_s[n * GDN_GROUP + g] = jnp.broadcast_to(jnp.exp(glast[g]), (8, LANES))
        return 0

    lax.fori_loop(0, S // C // GDN_GROUP, local, 0)

    state = jnp.zeros((HEAD_DIM, HEAD_DIM), F32)
    for n in range(S // C):
        sb = state.astype(BF16)
        st_s[n * C:(n + 1) * C, :] = sb
        state = (state * cd_s[n][0:1, :] + n_s[n * C:(n + 1) * C, :]
                 - dot(p_s[n * C:(n + 1) * C, :], sb))

    def emit(n, _):
        G = range(GDN_GROUP)
        sls = [pl.ds(pl.multiple_of((n * GDN_GROUP + g) * C, C), C) for g in G]
        outs = [dot(qp_s[sl, :], st_s[sl, :]) + oo_s[sl, :] for sl in sls]
        for sl, out in zip(sls, outs):
            z = z_ref[0, sl, :].astype(F32)
            o_ref[0, sl, :] = (_rms(out, on_ref[...]) * (z * _sigmoid(z))).astype(BF16)
        return 0

    lax.fori_loop(0, S // C // GDN_GROUP, emit, 0)


def _gdn_masks():
    C = GDN_C
    i = np.arange(C)[:, None]
    j = np.arange(C)[None, :]
    mats = [(i >= j), (i == j)]
    s = 1
    while s < C:
        mats.append(((i // (2 * s)) == (j // (2 * s))) & ((i // s) % 2 == 1) & ((j // s) % 2 == 0))
        s *= 2
    return jnp.asarray(np.stack(mats).astype(np.float32))


def _gdn(p_main3, gates, conv_w, out_norm, masks):
    B, S, _ = p_main3.shape
    assert S % (GDN_C * GDN_GROUP) == 0, S
    nm = masks.shape[0]
    cb = lambda base: (lambda b, h: (b, 0, base // LANES + h))
    cw = lambda base: (lambda b, h: (0, base // LANES + h))
    seq = lambda dt=F32: pltpu.VMEM((S, LANES), dt)
    return pl.pallas_call(
        _gdn_kernel,
        grid=(B, HEADS),
        in_specs=[
            pl.BlockSpec((1, S, LANES), cb(C_GQ)),
            pl.BlockSpec((1, S, LANES), cb(C_GK)),
            pl.BlockSpec((1, S, LANES), cb(C_GV)),
            pl.BlockSpec((1, S, LANES), cb(C_GZ)),
            pl.BlockSpec((1, S, LANES), lambda b, h: (b, 0, 0)),
            pl.BlockSpec((GDN_CONV, LANES), cw(0)),
            pl.BlockSpec((GDN_CONV, LANES), cw(BRANCH_W)),
            pl.BlockSpec((GDN_CONV, LANES), cw(2 * BRANCH_W)),
            pl.BlockSpec((1, LANES), lambda b, h: (0, 0)),
            pl.BlockSpec((nm, GDN_C, GDN_C), lambda b, h: (0, 0, 0)),
        ],
        out_specs=pl.BlockSpec((1, S, LANES), lambda b, h: (b, 0, h)),
        out_shape=jax.ShapeDtypeStruct((B, S, BRANCH_W), BF16),
        scratch_shapes=[seq() for _ in range(5)]
        + [seq(BF16), seq(), seq(BF16), seq(), seq(BF16)]
        + [pltpu.VMEM((S // GDN_C, 8, LANES), F32), pltpu.VMEM((S + 8, LANES), F32)],
        compiler_params=_cparams(("parallel", "arbitrary")),
        name="gdn",
    )(p_main3, p_main3, p_main3, p_main3, gates, conv_w, conv_w, conv_w, out_norm, masks)


def _merge_kernel(of_ref, om_ref, og_ref, wb_ref, g0_ref, g1_ref, g2_ref, o_ref):
    acc = None
    for o, g, br in ((of_ref, g0_ref, 0), (om_ref, g1_ref, 1), (og_ref, g2_ref, 2)):
        y = _sigmoid(g[...].astype(F32)) * jnp.dot(o[...], wb_ref[br], preferred_element_type=F32)
        acc = y if acc is None else acc + y
    o_ref[...] = acc.astype(BF16)


def _merge(o_fox, o_mla, o_gdn, w_branch, p_main, tm=1024, tn=512):
    T = o_fox.shape[0]
    osp = pl.BlockSpec((tm, BRANCH_W), lambda i, j: (i, 0))
    gsp = lambda br: pl.BlockSpec((tm, tn), lambda i, j: (i, (C_GATE + br * D_MODEL) // tn + j))
    return pl.pallas_call(
        _merge_kernel,
        grid=(T // tm, D_MODEL // tn),
        in_specs=[osp, osp, osp,
                  pl.BlockSpec((3, BRANCH_W, tn), lambda i, j: (0, 0, j)),
                  gsp(0), gsp(1), gsp(2)],
        out_specs=pl.BlockSpec((tm, tn), lambda i, j: (i, j)),
        out_shape=jax.ShapeDtypeStruct((T, D_MODEL), BF16),
        compiler_params=_cparams(("parallel", "arbitrary")),
        name="merge",
    )(o_fox, o_mla, o_gdn, w_branch, p_main, p_main, p_main)


def _outproj_kernel(m_ref, w_ref, x_ref, o_ref):
    o_ref[...] = x_ref[...] + jnp.dot(m_ref[...], w_ref[...], preferred_element_type=F32)


def _outproj(merged, w_out, x2, tm=1024, tn=1024):
    T = merged.shape[0]
    return pl.pallas_call(
        _outproj_kernel,
        grid=(T // tm, D_MODEL // tn),
        in_specs=[
            pl.BlockSpec((tm, D_MODEL), lambda i, j: (i, 0)),
            pl.BlockSpec((D_MODEL, tn), lambda i, j: (0, j)),
            pl.BlockSpec((tm, tn), lambda i, j: (i, j)),
        ],
        out_specs=pl.BlockSpec((tm, tn), lambda i, j: (i, j)),
        out_shape=jax.ShapeDtypeStruct((T, D_MODEL), F32),
        compiler_params=_cparams(("parallel", "arbitrary")),
        name="outproj",
    )(merged, w_out, x2)


def _ffn_kernel(x_ref, g_ref, wu_ref, wd_ref, fg_ref, o_ref, h_ref, *, final):
    f = pl.program_id(1)

    @pl.when(f == 0)
    def _():
        x = x_ref[...]
        h_ref[...] = _rms(x, g_ref[...]).astype(BF16)
        o_ref[...] = x

    hid = jnp.maximum(jnp.dot(h_ref[...], wu_ref[...], preferred_element_type=F32), 0.0)
    o_ref[...] += jnp.dot((hid * hid).astype(BF16), wd_ref[...], preferred_element_type=F32)

    if final:
        @pl.when(f == pl.num_programs(1) - 1)
        def _():
            o_ref[...] = _rms(o_ref[...], fg_ref[...])


def _ffn(x2, gain, w_up, w_down, final_gain, final, tm=1024, tf=512):
    T = x2.shape[0]
    return pl.pallas_call(
        functools.partial(_ffn_kernel, final=final),
        grid=(T // tm, D_FF // tf),
        in_specs=[
            pl.BlockSpec((tm, D_MODEL), lambda i, f: (i, 0), pipeline_mode=pl.Buffered(1)),
            pl.BlockSpec((1, D_MODEL), lambda i, f: (0, 0)),
            pl.BlockSpec((D_MODEL, tf), lambda i, f: (0, f)),
            pl.BlockSpec((tf, D_MODEL), lambda i, f: (f, 0)),
            pl.BlockSpec((1, D_MODEL), lambda i, f: (0, 0)),
        ],
        out_specs=pl.BlockSpec((tm, D_MODEL), lambda i, f: (i, 0)),
        out_shape=jax.ShapeDtypeStruct((T, D_MODEL), F32),
        scratch_shapes=[pltpu.VMEM((tm, D_MODEL), BF16)],
        compiler_params=_cparams(("parallel", "arbitrary")),
        name="ffn",
    )(x2, gain, w_up, w_down, final_gain)


def _rope_tables(S):
    inv_freq = ROPE_THETA ** (-jnp.arange(0, MLA_ROPE, 2, dtype=F32) / MLA_ROPE)
    ang = jnp.arange(S, dtype=F32)[:, None] * inv_freq[None, :]
    cos, sin = jnp.cos(ang), jnp.sin(ang)
    zeros = jnp.zeros((S, LANES - MLA_ROPE), F32)
    return (jnp.concatenate([cos, cos, zeros], axis=1),
            jnp.concatenate([-sin, sin, zeros], axis=1))


def _swap_halves(w):
    half = MLA_ROPE // 2
    return jnp.concatenate([w[..., half:], w[..., :half]], axis=-1)


def _split_w_in(w):
    sizes = (BRANCH_W, BRANCH_W, BRANCH_W, HEADS, MLA_RANK, MLA_RANK, MLA_ROPE,
             3 * BRANCH_W, BRANCH_W, HEADS, HEADS, 3 * D_MODEL)
    offs = np.cumsum((0,) + sizes)
    seg = lambda n: w[:, offs[n]:offs[n + 1]]
    fq, fk, fv, ff, cq, ckv, kpe, gqkv, gz, gb, ga, gate = (seg(n) for n in range(12))
    main = jnp.concatenate([fq, fk, cq, ckv, gqkv, gz, gate], axis=1).astype(BF16)
    z = lambda n: jnp.zeros((D_MODEL, n), F32)
    small = jnp.concatenate(
        [kpe, z(LANES - MLA_ROPE), _swap_halves(kpe), z(LANES - MLA_ROPE),
         ff, gb, ga, z(LANES - 3 * HEADS)], axis=1).astype(BF16)
    return main, small, fv.T.astype(BF16)


def _split_mla(w_uq, w_ukv):
    nope, pe = w_uq[..., :HEAD_DIM], w_uq[..., HEAD_DIM:]
    zpad = jnp.zeros((MLA_RANK, HEADS, LANES - MLA_ROPE), F32)
    wqa = jnp.concatenate([nope, pe, zpad], axis=-1).reshape(MLA_RANK, HEADS * MLA_QK)
    wqb = jnp.concatenate([_swap_halves(pe), zpad], axis=-1).reshape(MLA_RANK, HEADS * LANES)
    wk = w_ukv[..., :HEAD_DIM].reshape(MLA_RANK, BRANCH_W)
    wvt = w_ukv[..., HEAD_DIM:].reshape(MLA_RANK, BRANCH_W).T
    return tuple(a.astype(BF16) for a in (wqa, wqb, wk, wvt))


@jax.jit
def _forward(x, attn_norm, w_in, fox_fgate_bias, mla_q_norm, mla_kv_norm, w_mla_uq, w_mla_ukv,
             gdn_conv, gdn_a_log, gdn_dt_bias, gdn_out_norm, w_branch, w_out, mlp_norm,
             w_up, w_down, final_norm):
    B, S, D = x.shape
    T = B * S
    depth = w_in.shape[0]
    cos_t, sin_t = _rope_tables(S)
    masks = _gdn_masks()
    row = lambda v: v.reshape(1, -1).astype(F32)
    x2 = x.reshape(T, D)
    for l in range(depth):
        w_main, w_small, w_fvt = _split_w_in(w_in[l])
        p_main, p_small, fv_t = _proj_in(x2, row(attn_norm[l]), w_main, w_small, w_fvt, S)
        pm3 = p_main.reshape(B, S, N_MAIN)
        ps3 = p_small.reshape(B, S, N_SMALL)

        bias_row = (jnp.zeros((1, LANES), F32).at[0, S_FF:S_FF + HEADS].set(fox_fgate_bias[l])
                    .at[0, S_GA:S_GA + HEADS].set(gdn_dt_bias[l]))
        alog_row = jnp.zeros((1, LANES), F32).at[0, S_GA:S_GA + HEADS].set(gdn_a_log[l])
        cum_c, cum_r, gdn_gates = _gates(ps3, bias_row, alog_row)
        cum = (cum_c, cum_r)
        o_fox = _attention(pm3, pm3, fv_t, C_FQ // BRANCH_W, C_FK // BRANCH_W,
                           HEAD_DIM, HEAD_DIM ** -0.5, cum=cum)

        wqa, wqb, wk, wvt = _split_mla(w_mla_uq[l], w_mla_ukv[l])
        qf, kf, mv_t = _mla_prep(p_main, p_small, row(mla_q_norm[l]), row(mla_kv_norm[l]),
                                 wqa, wqb, wk, wvt, cos_t, sin_t, S)
        shp = lambda a: a.reshape(B, S, a.shape[-1])
        o_mla = _attention(shp(qf), shp(kf), mv_t, 0, 0, MLA_QK, (HEAD_DIM + MLA_ROPE) ** -0.5)

        o_gdn = _gdn(pm3, gdn_gates, gdn_conv[l], row(gdn_out_norm[l]), masks)

        merged = _merge(o_fox.reshape(T, BRANCH_W), o_mla.reshape(T, BRANCH_W),
                        o_gdn.reshape(T, BRANCH_W), w_branch[l].astype(BF16), p_main)
        x2 = _outproj(merged, w_out[l].astype(BF16), x2)
        x2 = _ffn(x2, row(mlp_norm[l]), w_up[l].astype(BF16), w_down[l].astype(BF16),
                  row(final_norm), final=(l == depth - 1))
    return x2.reshape(B, S, D)


def kernel(x, attn_norm, w_in, fox_fgate_bias, mla_q_norm, mla_kv_norm, w_mla_uq, w_mla_ukv,
           gdn_conv, gdn_a_log, gdn_dt_bias, gdn_out_norm, w_branch, w_out, mlp_norm,
           w_up, w_down, final_norm):
    return _forward(x, attn_norm, w_in, fox_fgate_bias, mla_q_norm, mla_kv_norm, w_mla_uq,
                    w_mla_ukv, gdn_conv, gdn_a_log, gdn_dt_bias, gdn_out_norm, w_branch, w_out,
                    mlp_norm, w_up, w_down, final_norm)
```

```python
import functools
import math

import numpy as np
import jax
import jax.numpy as jnp
from jax import lax
from jax.experimental import pallas as pl
from jax.experimental.pallas import tpu as pltpu

F32 = jnp.float32
BF16 = jnp.bfloat16
HIGHEST = lax.Precision.HIGHEST

D_MODEL = 2048
HEAD_DIM = 128
HEADS = 8
BRANCH_W = HEADS * HEAD_DIM
MLA_RANK = 512
MLA_ROPE = 64
MLA_QK = 256
ROPE_THETA = 10000.0
GDN_CONV = 4
D_FF = 4 * D_MODEL
EPS = 1e-6
LOG2E = math.log2(math.e)

LANES = 128
VMEM_LIMIT = 52 * 1024 * 1024

C_FQ, C_FK = 0, 1024
C_CQ, C_CKV = 2048, 2560
C_GQ, C_GK, C_GV, C_GZ = 3072, 4096, 5120, 6144
C_GATE = 7168
N_MAIN = C_GATE + 3 * D_MODEL
N_SMALL = 3 * LANES
S_FF, S_GB, S_GA = 0, 8, 16

GDN_C = 128
ATTN_GROUP = 8
GDN_HP = 2
GDN_GROUP = 8


def _cparams(sem):
    return pltpu.CompilerParams(dimension_semantics=sem, vmem_limit_bytes=VMEM_LIMIT)


def _sigmoid(x):
    return 0.5 * jnp.tanh(0.5 * x) + 0.5


def _softplus(x):
    return jnp.maximum(x, 0.0) + jnp.log(1.0 + jnp.exp(-jnp.abs(x)))


def _rms(x, gain):
    return x * lax.rsqrt(jnp.mean(x * x, axis=-1, keepdims=True) + EPS) * gain


def _proj_kernel(x_ref, g_ref, w_ref, ws_ref, wvt_ref, o_ref, os_ref, vt_ref, u_ref):
    @pl.when(pl.program_id(1) == 0)
    def _():
        u = _rms(x_ref[...], g_ref[...]).astype(BF16)
        u_ref[...] = u
        os_ref[...] = jnp.dot(u, ws_ref[...], preferred_element_type=F32)
        vt_ref[0] = lax.dot_general(wvt_ref[...], u, (((1,), (1,)), ((), ())),
                                    preferred_element_type=F32).astype(BF16)

    o_ref[...] = jnp.dot(u_ref[...], w_ref[...], preferred_element_type=F32).astype(BF16)


def _proj_in(x2, gain, w_main, w_small, w_fvt, S, tm=1024, tn=1024):
    T = x2.shape[0]
    nS = S // tm
    return pl.pallas_call(
        _proj_kernel,
        grid=(T // tm, N_MAIN // tn),
        in_specs=[
            pl.BlockSpec((tm, D_MODEL), lambda i, j: (i, 0), pipeline_mode=pl.Buffered(1)),
            pl.BlockSpec((1, D_MODEL), lambda i, j: (0, 0)),
            pl.BlockSpec((D_MODEL, tn), lambda i, j: (0, j)),
            pl.BlockSpec((D_MODEL, N_SMALL), lambda i, j: (0, 0)),
            pl.BlockSpec((BRANCH_W, D_MODEL), lambda i, j: (0, 0)),
        ],
        out_specs=[
            pl.BlockSpec((tm, tn), lambda i, j: (i, j)),
            pl.BlockSpec((tm, N_SMALL), lambda i, j: (i, 0)),
            pl.BlockSpec((1, BRANCH_W, tm), lambda i, j: (i // nS, 0, i % nS)),
        ],
        out_shape=[
            jax.ShapeDtypeStruct((T, N_MAIN), BF16),
            jax.ShapeDtypeStruct((T, N_SMALL), F32),
            jax.ShapeDtypeStruct((T // S, BRANCH_W, S), BF16),
        ],
        scratch_shapes=[pltpu.VMEM((tm, D_MODEL), BF16)],
        compiler_params=_cparams(("parallel", "arbitrary")),
        name="proj_in",
    )(x2, gain, w_main, w_small, w_fvt)


def _cumsum_kernel(s_ref, b_ref, al_ref, c_ref, ct_ref, gt_ref):
    S = s_ref.shape[1]
    x = s_ref[0] + b_ref[...]
    lane = lax.broadcasted_iota(jnp.int32, (S, LANES), 1)
    gt_ref[0] = jnp.where(lane < S_GA, _sigmoid(x), -jnp.exp(al_ref[...]) * _softplus(x))
    logf = -_softplus(-x) * LOG2E
    r = lax.broadcasted_iota(jnp.int32, (LANES, LANES), 0)
    c = lax.broadcasted_iota(jnp.int32, (LANES, LANES), 1)
    tri = (r >= c).astype(F32)
    carry = jnp.zeros((1, LANES), F32)
    for blk in range(S // LANES):
        seg = logf[blk * LANES:(blk + 1) * LANES]
        cs = jnp.dot(tri, seg, precision=HIGHEST, preferred_element_type=F32) + carry
        c_ref[0, blk * LANES:(blk + 1) * LANES, :] = cs
        carry = cs[LANES - 1:LANES, :]
    ct_ref[0] = c_ref[0].T[:HEADS]


def _gates(small3, bias_row, alog_row):
    B, S, _ = small3.shape
    return pl.pallas_call(
        _cumsum_kernel,
        grid=(B,),
        in_specs=[
            pl.BlockSpec((1, S, LANES), lambda b: (b, 0, 2)),
            pl.BlockSpec((1, LANES), lambda b: (0, 0)),
            pl.BlockSpec((1, LANES), lambda b: (0, 0)),
        ],
        out_specs=[
            pl.BlockSpec((1, S, LANES), lambda b: (b, 0, 0)),
            pl.BlockSpec((1, HEADS, S), lambda b: (b, 0, 0)),
            pl.BlockSpec((1, S, LANES), lambda b: (b, 0, 0)),
        ],
        out_shape=[
            jax.ShapeDtypeStruct((B, S, LANES), F32),
            jax.ShapeDtypeStruct((B, HEADS, S), F32),
            jax.ShapeDtypeStruct((B, S, LANES), F32),
        ],
        compiler_params=_cparams(("parallel",)),
        name="gates",
    )(small3, bias_row, alog_row)


def _attn_kernel(*refs, dk, scale, decay, tq):
    if decay:
        q_ref, k_ref, vt_ref, cq_ref, ck_ref, o_ref = refs
    else:
        q_ref, k_ref, vt_ref, o_ref = refs
    i = pl.program_id(1)
    krow = lax.broadcasted_iota(jnp.int32, (tq, tq), 0)
    qcol = lax.broadcasted_iota(jnp.int32, (tq, tq), 1)
    causal = krow <= qcol
    c2 = scale * LOG2E

    def group_step(hs, qs, cqs, j, carry, masked):
        off = pl.multiple_of(j * tq, tq)
        ts = []
        for h, q in zip(hs, qs):
            k = k_ref[0, pl.ds(off, tq), h * dk:(h + 1) * dk]
            t = lax.dot_general(k, q, (((1,), (1,)), ((), ())), preferred_element_type=F32) * c2
            if decay:
                t = t - ck_ref[0, pl.ds(off, tq), h:h + 1]
            if masked:
                t = jnp.where(causal, t, -jnp.inf)
            ts.append(t)
        ps = []
        for t, cq, (m, l, acc) in zip(ts, cqs, carry):
            mt = jnp.max(t, axis=0, keepdims=True)
            m_new = jnp.maximum(m, mt + cq if decay else mt)
            p = jnp.exp2(t - (m_new - cq if decay else m_new))
            alpha = jnp.exp2(m - m_new)
            ps.append((m_new, alpha, alpha * l + jnp.sum(p, axis=0, keepdims=True), p.astype(BF16)))
        out = []
        for h, (m_new, alpha, l, p), (_, _, acc) in zip(hs, ps, carry):
            vt = vt_ref[0, h * HEAD_DIM:(h + 1) * HEAD_DIM, pl.ds(off, tq)]
            out.append((m_new, l, alpha * acc + jnp.dot(vt, p, preferred_element_type=F32)))
        return tuple(out)

    for h0 in range(0, HEADS, ATTN_GROUP):
        hs = tuple(range(h0, h0 + ATTN_GROUP))
        qs = [q_ref[0, :, h * dk:(h + 1) * dk] for h in hs]
        cqs = [cq_ref[0, h:h + 1, :] if decay else None for h in hs]

        def step(j, carry, masked, hs=hs, qs=qs, cqs=cqs):
            return group_step(hs, qs, cqs, j, carry, masked)

        init = tuple((jnp.full((1, tq), -jnp.inf, F32), jnp.zeros((1, tq), F32),
                      jnp.zeros((HEAD_DIM, tq), F32)) for _ in hs)
        carry = lax.fori_loop(0, i, functools.partial(step, masked=False), init)
        for h, (_, l, acc) in zip(hs, step(i, carry, masked=True)):
            o_ref[0, :, h * HEAD_DIM:(h + 1) * HEAD_DIM] = (acc / l).T.astype(BF16)


def _attention(q_arr, k_arr, vt_arr, q_blk, k_blk, dk, scale, cum=None, tq=512):
    B, S, _ = q_arr.shape
    decay = cum is not None
    in_specs = [
        pl.BlockSpec((1, tq, HEADS * dk), lambda b, i: (b, i, q_blk)),
        pl.BlockSpec((1, S, HEADS * dk), lambda b, i: (b, 0, k_blk)),
        pl.BlockSpec((1, BRANCH_W, S), lambda b, i: (b, 0, 0)),
    ]
    args = [q_arr, k_arr, vt_arr]
    if decay:
        in_specs += [
            pl.BlockSpec((1, HEADS, tq), lambda b, i: (b, 0, i)),
            pl.BlockSpec((1, S, LANES), lambda b, i: (b, 0, 0)),
        ]
        args += [cum[1], cum[0]]
    return pl.pallas_call(
        functools.partial(_attn_kernel, dk=dk, scale=scale, decay=decay, tq=tq),
        grid=(B, S // tq),
        in_specs=in_specs,
        out_specs=pl.BlockSpec((1, tq, BRANCH_W), lambda b, i: (b, i, 0)),
        out_shape=jax.ShapeDtypeStruct((B, S, BRANCH_W), BF16),
        compiler_params=_cparams(("parallel", "arbitrary")),
        name="fox_attn" if decay else "mla_attn",
    )(*args)


def _mla_prep_kernel(cq_ref, ckv_ref, sm_ref, qn_ref, kvn_ref, wqa_ref, wqb_ref, wk_ref, wv_ref,
                     cos_ref, sin_ref, q_ref, k_ref, v_ref):
    cos = cos_ref[...]
    sin = sin_ref[...]
    qn = _rms(cq_ref[...].astype(F32), qn_ref[...]).astype(BF16)
    a = jnp.dot(qn, wqa_ref[...], preferred_element_type=F32)
    bsw = jnp.dot(qn, wqb_ref[...], preferred_element_type=F32)
    kvn = _rms(ckv_ref[...].astype(F32), kvn_ref[...]).astype(BF16)
    kn = jnp.dot(kvn, wk_ref[...], preferred_element_type=F32)
    v_ref[0] = lax.dot_general(wv_ref[...], kvn, (((1,), (1,)), ((), ())),
                               preferred_element_type=F32).astype(BF16)
    sm = sm_ref[...]
    kpe = (sm[:, 0:LANES] * cos + sm[:, LANES:2 * LANES] * sin).astype(BF16)
    for h in range(HEADS):
        lo = h * MLA_QK
        q_ref[:, lo:lo + LANES] = a[:, lo:lo + LANES].astype(BF16)
        q_ref[:, lo + LANES:lo + 2 * LANES] = (
            a[:, lo + LANES:lo + 2 * LANES] * cos + bsw[:, h * LANES:(h + 1) * LANES] * sin
        ).astype(BF16)
        k_ref[:, lo:lo + LANES] = kn[:, h * LANES:(h + 1) * LANES].astype(BF16)
        k_ref[:, lo + LANES:lo + 2 * LANES] = kpe


def _mla_prep(p_main, p_small, qn, kvn, wqa, wqb, wk, wv, cos_t, sin_t, S, tm=512):
    T = p_main.shape[0]
    nS = S // tm
    const = lambda i: (0, 0)
    return pl.pallas_call(
        _mla_prep_kernel,
        grid=(T // tm,),
        in_specs=[
            pl.BlockSpec((tm, MLA_RANK), lambda i: (i, C_CQ // MLA_RANK)),
            pl.BlockSpec((tm, MLA_RANK), lambda i: (i, C_CKV // MLA_RANK)),
            pl.BlockSpec((tm, N_SMALL), lambda i: (i, 0)),
            pl.BlockSpec((1, MLA_RANK), const),
            pl.BlockSpec((1, MLA_RANK), const),
            pl.BlockSpec((MLA_RANK, HEADS * MLA_QK), const),
            pl.BlockSpec((MLA_RANK, HEADS * LANES), const),
            pl.BlockSpec((MLA_RANK, BRANCH_W), const),
            pl.BlockSpec((BRANCH_W, MLA_RANK), const),
            pl.BlockSpec((tm, LANES), lambda i: (i % nS, 0)),
            pl.BlockSpec((tm, LANES), lambda i: (i % nS, 0)),
        ],
        out_specs=[
            pl.BlockSpec((tm, HEADS * MLA_QK), lambda i: (i, 0)),
            pl.BlockSpec((tm, HEADS * MLA_QK), lambda i: (i, 0)),
            pl.BlockSpec((1, BRANCH_W, tm), lambda i: (i // nS, 0, i % nS)),
        ],
        out_shape=[
            jax.ShapeDtypeStruct((T, HEADS * MLA_QK), BF16),
            jax.ShapeDtypeStruct((T, HEADS * MLA_QK), BF16),
            jax.ShapeDtypeStruct((T // S, BRANCH_W, S), BF16),
        ],
        compiler_params=_cparams(("parallel",)),
        name="mla_prep",
    )(p_main, p_main, p_small, qn, kvn, wqa, wqb, wk, wv, cos_t, sin_t)


def _gdn_kernel(q_ref, k_ref, v_ref, z_ref, gt_ref, cwq_ref, cwk_ref, cwv_ref, on_ref,
                mk_ref, o_ref, qn_s, kn_s, vn_s, g_s, b_s, p_s, n_s, qp_s, oo_s, st_s, cd_s, xp_s):
    h0 = pl.program_id(1) * GDN_HP
    S = q_ref.shape[1]
    C = GDN_C
    NC = S // C
    lane = lax.broadcasted_iota(jnp.int32, (S, LANES), 1)

    xp_s[0:8, :] = jnp.zeros((8, LANES), F32)

    def conv_silu(x_ref, cw_ref, hh):
        cols = pl.ds(pl.multiple_of(hh * LANES, LANES), LANES)
        w = cw_ref[:, cols] * 0.5
        xp_s[8:, :] = x_ref[0, :, cols].astype(F32)
        hy = None
        for sft in range(GDN_CONV):
            t = xp_s[8 - sft:8 - sft + S, :] * w[GDN_CONV - 1 - sft:GDN_CONV - sft]
            hy = t if hy is None else hy + t
        return hy + hy * jnp.tanh(hy)

    def l2n(x):
        return x * lax.rsqrt(jnp.sum(x * x, axis=-1, keepdims=True) + EPS)

    def prologue(hh, _):
        rows = pl.ds(pl.multiple_of(hh * S, S), S)
        qn_s[rows, :] = l2n(conv_silu(q_ref, cwq_ref, hh)) * (HEAD_DIM ** -0.5)
        kn_s[rows, :] = l2n(conv_silu(k_ref, cwk_ref, hh))
        vn_s[rows, :] = conv_silu(v_ref, cwv_ref, hh)
        gates = gt_ref[0]
        pick = lambda col: jnp.broadcast_to(
            jnp.sum(jnp.where(lane == col + h0 + hh, gates, 0.0), axis=-1, keepdims=True),
            (S, LANES))
        b_s[rows, :] = pick(S_GB)
        g_s[rows, :] = pick(S_GA)
        return 0

    lax.fori_loop(0, GDN_HP, prologue, 0)

    tri = mk_ref[0]
    eye = mk_ref[1]
    incl = tri > 0.5
    strict = tri - eye

    nt = (((1,), (1,)), ((), ()))
    dot = functools.partial(jnp.dot, preferred_element_type=F32)

    def local(n, _):
        G = range(GDN_GROUP)
        sls = [pl.ds(pl.multiple_of((n * GDN_GROUP + g) * C, C), C) for g in G]
        q = [qn_s[sl, :] for sl in sls]
        k = [kn_s[sl, :] for sl in sls]
        beta = [b_s[sl, :] for sl in sls]
        gc = [jnp.dot(tri, g_s[sl, :], precision=HIGHEST, preferred_element_type=F32) for sl in sls]
        decay = [jnp.exp(jnp.where(incl, x - x.T, -jnp.inf)) for x in gc]
        kb = [k[g] * beta[g] for g in G]
        kbf = [x.astype(BF16) for x in k]
        a = [lax.dot_general(kb[g].astype(BF16), kbf[g], nt, preferred_element_type=F32)
             * decay[g] * strict for g in G]
        intra = [(lax.dot_general(q[g].astype(BF16), kbf[g], nt, preferred_element_type=F32)
                  * decay[g]).astype(BF16) for g in G]
        t = [eye - a[g] * mk_ref[2] for g in G]
        for lvl in range(1, int(math.log2(C))):
            tb = [x.astype(BF16) for x in t]
            tc = [dot(tb[g], (a[g] * mk_ref[2 + lvl]).astype(BF16)).astype(BF16) for g in G]
            t = [t[g] - dot(tc[g], tb[g]) for g in G]
        tb = [x.astype(BF16) for x in t]
        eg = [jnp.exp(x) for x in gc]
        ub = [dot(tb[g], (vn_s[sls[g], :] * beta[g]).astype(BF16)).astype(BF16) for g in G]
        wb = [dot(tb[g], (kb[g] * eg[g]).astype(BF16)).astype(BF16) for g in G]
        glast = [x[C - 1:C, :] for x in gc]
        kdt = [(k[g] * jnp.exp(glast[g] - gc[g])).T.astype(BF16) for g in G]
        for g in G:
            p_s[sls[g], :] = dot(kdt[g], wb[g]).astype(BF16)
        for g in G:
            n_s[sls[g], :] = dot(kdt[g], ub[g])
        for g in G:
            qp_s[sls[g], :] = (q[g] * eg[g] - dot(intra[g], wb[g])).astype(BF16)
        for g in G:
            oo_s[sls[g], :] = dot(intra[g], ub[g])
            cd_s[n * GDN_GROUP + g] = jnp.broadcast_to(jnp.exp(glast[g]), (8, LANES))
        return 0

    lax.fori_loop(0, GDN_HP * NC // GDN_GROUP, local, 0)

    states = [jnp.zeros((HEAD_DIM, HEAD_DIM), F32) for _ in range(GDN_HP)]
    for n in range(NC):
        rows = [slice((hh * NC + n) * C, (hh * NC + n + 1) * C) for hh in range(GDN_HP)]
        sbs = [s.astype(BF16) for s in states]
        for r, sb in zip(rows, sbs):
            st_s[r, :] = sb
        ps = [dot(p_s[r, :], sb) for r, sb in zip(rows, sbs)]
        states = [states[hh] * cd_s[hh * NC + n][0:1, :] + n_s[rows[hh], :] - ps[hh]
                  for hh in range(GDN_HP)]

    for hh in range(GDN_HP):
        def emit(n, _, hh=hh):
            G = range(GDN_GROUP)
            offs = [pl.multiple_of((n * GDN_GROUP + g) * C, C) for g in G]
            sls = [pl.ds(hh * S + off, C) for off in offs]
            outs = [dot(qp_s[sl, :], st_s[sl, :]) + oo_s[sl, :] for sl in sls]
            for off, out in zip(offs, outs):
                z = z_ref[0, pl.ds(off, C), hh * LANES:(hh + 1) * LANES].astype(F32)
                o_ref[0, pl.ds(off, C), hh * LANES:(hh + 1) * LANES] = (
                    _rms(out, on_ref[...]) * (z * _sigmoid(z))).astype(BF16)
            return 0

        lax.fori_loop(0, NC // GDN_GROUP, emit, 0)


def _gdn_masks():
    C = GDN_C
    i = np.arange(C)[:, None]
    j = np.arange(C)[None, :]
    mats = [(i >= j), (i == j)]
    s = 1
    while s < C:
        mats.append(((i // (2 * s)) == (j // (2 * s))) & ((i // s) % 2 == 1) & ((j // s) % 2 == 0))
        s *= 2
    return jnp.asarray(np.stack(mats).astype(np.float32))


def _gdn(p_main3, gates, conv_w, out_norm, masks):
    B, S, _ = p_main3.shape
    assert S % (GDN_C * GDN_GROUP) == 0, S
    nm = masks.shape[0]
    W = GDN_HP * LANES
    cb = lambda base: (lambda b, h: (b, 0, base // W + h))
    cw = lambda base: (lambda b, h: (0, base // W + h))
    seq = lambda dt=F32: pltpu.VMEM((GDN_HP * S, LANES), dt)
    return pl.pallas_call(
        _gdn_kernel,
        grid=(B, HEADS // GDN_HP),
        in_specs=[
            pl.BlockSpec((1, S, W), cb(C_GQ)),
            pl.BlockSpec((1, S, W), cb(C_GK)),
            pl.BlockSpec((1, S, W), cb(C_GV)),
            pl.BlockSpec((1, S, W), cb(C_GZ)),
            pl.BlockSpec((1, S, LANES), lambda b, h: (b, 0, 0)),
            pl.BlockSpec((GDN_CONV, W), cw(0)),
            pl.BlockSpec((GDN_CONV, W), cw(BRANCH_W)),
            pl.BlockSpec((GDN_CONV, W), cw(2 * BRANCH_W)),
            pl.BlockSpec((1, LANES), lambda b, h: (0, 0)),
            pl.BlockSpec((nm, GDN_C, GDN_C), lambda b, h: (0, 0, 0)),
        ],
        out_specs=pl.BlockSpec((1, S, W), lambda b, h: (b, 0, h)),
        out_shape=jax.ShapeDtypeStruct((B, S, BRANCH_W), BF16),
        scratch_shapes=[seq() for _ in range(5)]
        + [seq(BF16), seq(), seq(BF16), seq(), seq(BF16)]
        + [pltpu.VMEM((GDN_HP * S // GDN_C, 8, LANES), F32), pltpu.VMEM((S + 8, LANES), F32)],
        compiler_params=_cparams(("parallel", "arbitrary")),
        name="gdn",
    )(p_main3, p_main3, p_main3, p_main3, gates, conv_w, conv_w, conv_w, out_norm, masks)


def _merge_kernel(of_ref, om_ref, og_ref, wb_ref, g0_ref, g1_ref, g2_ref, o_ref):
    acc = None
    for o, g, br in ((of_ref, g0_ref, 0), (om_ref, g1_ref, 1), (og_ref, g2_ref, 2)):
        y = _sigmoid(g[...].astype(F32)) * jnp.dot(o[...], wb_ref[br], preferred_element_type=F32)
        acc = y if acc is None else acc + y
    o_ref[...] = acc.astype(BF16)


def _merge(o_fox, o_mla, o_gdn, w_branch, p_main, tm=1024, tn=512):
    T = o_fox.shape[0]
    osp = pl.BlockSpec((tm, BRANCH_W), lambda i, j: (i, 0))
    gsp = lambda br: pl.BlockSpec((tm, tn), lambda i, j: (i, (C_GATE + br * D_MODEL) // tn + j))
    return pl.pallas_call(
        _merge_kernel,
        grid=(T // tm, D_MODEL // tn),
        in_specs=[osp, osp, osp,
                  pl.BlockSpec((3, BRANCH_W, tn), lambda i, j: (0, 0, j)),
                  gsp(0), gsp(1), gsp(2)],
        out_specs=pl.BlockSpec((tm, tn), lambda i, j: (i, j)),
        out_shape=jax.ShapeDtypeStruct((T, D_MODEL), BF16),
        compiler_params=_cparams(("parallel", "arbitrary")),
        name="merge",
    )(o_fox, o_mla, o_gdn, w_branch, p_main, p_main, p_main)


def _outproj_kernel(m_ref, w_ref, x_ref, o_ref):
    o_ref[...] = x_ref[...] + jnp.dot(m_ref[...], w_ref[...], preferred_element_type=F32)


def _outproj(merged, w_out, x2, tm=1024, tn=1024):
    T = merged.shape[0]
    return pl.pallas_call(
        _outproj_kernel,
        grid=(T // tm, D_MODEL // tn),
        in_specs=[
            pl.BlockSpec((tm, D_MODEL), lambda i, j: (i, 0)),
            pl.BlockSpec((D_MODEL, tn), lambda i, j: (0, j)),
            pl.BlockSpec((tm, tn), lambda i, j: (i, j)),
        ],
        out_specs=pl.BlockSpec((tm, tn), lambda i, j: (i, j)),
        out_shape=jax.ShapeDtypeStruct((T, D_MODEL), F32),
        compiler_params=_cparams(("parallel", "arbitrary")),
        name="outproj",
    )(merged, w_out, x2)


def _ffn_kernel(x_ref, g_ref, wu_ref, wd_ref, fg_ref, o_ref, h_ref, *, final):
    f = pl.program_id(1)

    @pl.when(f == 0)
    def _():
        x = x_ref[...]
        h_ref[...] = _rms(x, g_ref[...]).astype(BF16)
        o_ref[...] = x

    hid = jnp.maximum(jnp.dot(h_ref[...], wu_ref[...], preferred_element_type=F32), 0.0)
    o_ref[...] += jnp.dot((hid * hid).astype(BF16), wd_ref[...], preferred_element_type=F32)

    if final:
        @pl.when(f == pl.num_programs(1) - 1)
        def _():
            o_ref[...] = _rms(o_ref[...], fg_ref[...])


def _ffn(x2, gain, w_up, w_down, final_gain, final, tm=1024, tf=1024):
    T = x2.shape[0]
    return pl.pallas_call(
        functools.partial(_ffn_kernel, final=final),
        grid=(T // tm, D_FF // tf),
        in_specs=[
            pl.BlockSpec((tm, D_MODEL), lambda i, f: (i, 0), pipeline_mode=pl.Buffered(1)),
            pl.BlockSpec((1, D_MODEL), lambda i, f: (0, 0)),
            pl.BlockSpec((D_MODEL, tf), lambda i, f: (0, f)),
            pl.BlockSpec((tf, D_MODEL), lambda i, f: (f, 0)),
            pl.BlockSpec((1, D_MODEL), lambda i, f: (0, 0)),
        ],
        out_specs=pl.BlockSpec((tm, D_MODEL), lambda i, f: (i, 0)),
        out_shape=jax.ShapeDtypeStruct((T, D_MODEL), F32),
        scratch_shapes=[pltpu.VMEM((tm, D_MODEL), BF16)],
        compiler_params=_cparams(("parallel", "arbitrary")),
        name="ffn",
    )(x2, gain, w_up, w_down, final_gain)


def _rope_tables(S):
    inv_freq = ROPE_THETA ** (-jnp.arange(0, MLA_ROPE, 2, dtype=F32) / MLA_ROPE)
    ang = jnp.arange(S, dtype=F32)[:, None] * inv_freq[None, :]
    cos, sin = jnp.cos(ang), jnp.sin(ang)
    zeros = jnp.zeros((S, LANES - MLA_ROPE), F32)
    return (jnp.concatenate([cos, cos, zeros], axis=1),
            jnp.concatenate([-sin, sin, zeros], axis=1))


def _swap_halves(w):
    half = MLA_ROPE // 2
    return jnp.concatenate([w[..., half:], w[..., :half]], axis=-1)


def _split_w_in(w):
    sizes = (BRANCH_W, BRANCH_W, BRANCH_W, HEADS, MLA_RANK, MLA_RANK, MLA_ROPE,
             3 * BRANCH_W, BRANCH_W, HEADS, HEADS, 3 * D_MODEL)
    offs = np.cumsum((0,) + sizes)
    seg = lambda n: w[:, offs[n]:offs[n + 1]]
    fq, fk, fv, ff, cq, ckv, kpe, gqkv, gz, gb, ga, gate = (seg(n) for n in range(12))
    main = jnp.concatenate([fq, fk, cq, ckv, gqkv, gz, gate], axis=1).astype(BF16)
    z = lambda n: jnp.zeros((D_MODEL, n), F32)
    small = jnp.concatenate(
        [kpe, z(LANES - MLA_ROPE), _swap_halves(kpe), z(LANES - MLA_ROPE),
         ff, gb, ga, z(LANES - 3 * HEADS)], axis=1).astype(BF16)
    return main, small, fv.T.astype(BF16)


def _split_mla(w_uq, w_ukv):
    nope, pe = w_uq[..., :HEAD_DIM], w_uq[..., HEAD_DIM:]
    zpad = jnp.zeros((MLA_RANK, HEADS, LANES - MLA_ROPE), F32)
    wqa = jnp.concatenate([nope, pe, zpad], axis=-1).reshape(MLA_RANK, HEADS * MLA_QK)
    wqb = jnp.concatenate([_swap_halves(pe), zpad], axis=-1).reshape(MLA_RANK, HEADS * LANES)
    wk = w_ukv[..., :HEAD_DIM].reshape(MLA_RANK, BRANCH_W)
    wvt = w_ukv[..., HEAD_DIM:].reshape(MLA_RANK, BRANCH_W).T
    return tuple(a.astype(BF16) for a in (wqa, wqb, wk, wvt))


@jax.jit
def _forward(x, attn_norm, w_in, fox_fgate_bias, mla_q_norm, mla_kv_norm, w_mla_uq, w_mla_ukv,
             gdn_conv, gdn_a_log, gdn_dt_bias, gdn_out_norm, w_branch, w_out, mlp_norm,
             w_up, w_down, final_norm):
    B, S, D = x.shape
    T = B * S
    depth = w_in.shape[0]
    cos_t, sin_t = _rope_tables(S)
    masks = _gdn_masks()
    row = lambda v: v.reshape(1, -1).astype(F32)
    x2 = x.reshape(T, D)
    for l in range(depth):
        w_main, w_small, w_fvt = _split_w_in(w_in[l])
        p_main, p_small, fv_t = _proj_in(x2, row(attn_norm[l]), w_main, w_small, w_fvt, S)
        pm3 = p_main.reshape(B, S, N_MAIN)
        ps3 = p_small.reshape(B, S, N_SMALL)

        bias_row = (jnp.zeros((1, LANES), F32).at[0, S_FF:S_FF + HEADS].set(fox_fgate_bias[l])
                    .at[0, S_GA:S_GA + HEADS].set(gdn_dt_bias[l]))
        alog_row = jnp.zeros((1, LANES), F32).at[0, S_GA:S_GA + HEADS].set(gdn_a_log[l])
        cum_c, cum_r, gdn_gates = _gates(ps3, bias_row, alog_row)
        cum = (cum_c, cum_r)
        o_fox = _attention(pm3, pm3, fv_t, C_FQ // BRANCH_W, C_FK // BRANCH_W,
                           HEAD_DIM, HEAD_DIM ** -0.5, cum=cum)

        wqa, wqb, wk, wvt = _split_mla(w_mla_uq[l], w_mla_ukv[l])
        qf, kf, mv_t = _mla_prep(p_main, p_small, row(mla_q_norm[l]), row(mla_kv_norm[l]),
                                 wqa, wqb, wk, wvt, cos_t, sin_t, S)
        shp = lambda a: a.reshape(B, S, a.shape[-1])
        o_mla = _attention(shp(qf), shp(kf), mv_t, 0, 0, MLA_QK, (HEAD_DIM + MLA_ROPE) ** -0.5)

        o_gdn = _gdn(pm3, gdn_gates, gdn_conv[l], row(gdn_out_norm[l]), masks)

        merged = _merge(o_fox.reshape(T, BRANCH_W), o_mla.reshape(T, BRANCH_W),
                        o_gdn.reshape(T, BRANCH_W), w_branch[l].astype(BF16), p_main)
        x2 = _outproj(merged, w_out[l].astype(BF16), x2)
        x2 = _ffn(x2, row(mlp_norm[l]), w_up[l].astype(BF16), w_down[l].astype(BF16),
                  row(final_norm), final=(l == depth - 1))
    return x2.reshape(B, S, D)


def kernel(x, attn_norm, w_in, fox_fgate_bias, mla_q_norm, mla_kv_norm, w_mla_uq, w_mla_ukv,
           gdn_conv, gdn_a_log, gdn_dt_bias, gdn_out_norm, w_branch, w_out, mlp_norm,
           w_up, w_down, final_norm):
    return _forward(x, attn_norm, w_in, fox_fgate_bias, mla_q_norm, mla_kv_norm, w_mla_uq,
                    w_mla_ukv, gdn_conv, gdn_a_log, gdn_dt_bias, gdn_out_norm, w_branch, w_out,
                    mlp_norm, w_up, w_down, final_norm)
```

```python
import functools
import math

import numpy as np
import jax
import jax.numpy as jnp
from jax import lax
from jax.experimental import pallas as pl
from jax.experimental.pallas import tpu as pltpu

F32 = jnp.float32
BF16 = jnp.bfloat16
HIGHEST = lax.Precision.HIGHEST

D_MODEL = 2048
HEAD_DIM = 128
HEADS = 8
BRANCH_W = HEADS * HEAD_DIM
MLA_RANK = 512
MLA_ROPE = 64
MLA_QK = 256
ROPE_THETA = 10000.0
GDN_CONV = 4
D_FF = 4 * D_MODEL
EPS = 1e-6
LOG2E = math.log2(math.e)

LANES = 128
VMEM_LIMIT = 52 * 1024 * 1024

C_FQ, C_FK = 0, 1024
C_CQ, C_CKV = 2048, 2560
C_GQ, C_GK, C_GV, C_GZ = 3072, 4096, 5120, 6144
C_GATE = 7168
N_MAIN = C_GATE + 3 * D_MODEL
N_SMALL = 3 * LANES
S_FF, S_GB, S_GA = 0, 8, 16

GDN_C = 128
ATTN_GROUP = 8
GDN_HP = 2
GDN_GROUP = 8


def _cparams(sem):
    return pltpu.CompilerParams(dimension_semantics=sem, vmem_limit_bytes=VMEM_LIMIT)


def _sigmoid(x):
    return 0.5 * jnp.tanh(0.5 * x) + 0.5


def _softplus(x):
    return jnp.maximum(x, 0.0) + jnp.log(1.0 + jnp.exp(-jnp.abs(x)))


def _rms(x, gain):
    return x * lax.rsqrt(jnp.mean(x * x, axis=-1, keepdims=True) + EPS) * gain


def _proj_kernel(x_ref, g_ref, w_ref, ws_ref, wvt_ref, o_ref, os_ref, vt_ref, u_ref):
    @pl.when(pl.program_id(1) == 0)
    def _():
        u = _rms(x_ref[...], g_ref[...]).astype(BF16)
        u_ref[...] = u
        os_ref[...] = jnp.dot(u, ws_ref[...], preferred_element_type=F32)
        vt_ref[0] = lax.dot_general(wvt_ref[...], u, (((1,), (1,)), ((), ())),
                                    preferred_element_type=F32).astype(BF16)

    o_ref[...] = jnp.dot(u_ref[...], w_ref[...], preferred_element_type=F32).astype(BF16)


def _proj_in(x2, gain, w_main, w_small, w_fvt, l, S, tm=1024, tn=1024):
    T = x2.shape[0]
    nS = S // tm
    return pl.pallas_call(
        _proj_kernel,
        grid=(T // tm, N_MAIN // tn),
        in_specs=[
            pl.BlockSpec((tm, D_MODEL), lambda i, j: (i, 0), pipeline_mode=pl.Buffered(1)),
            pl.BlockSpec((1, D_MODEL), lambda i, j: (0, 0)),
            pl.BlockSpec((None, D_MODEL, tn), lambda i, j: (l, 0, j)),
            pl.BlockSpec((None, D_MODEL, N_SMALL), lambda i, j: (l, 0, 0)),
            pl.BlockSpec((None, BRANCH_W, D_MODEL), lambda i, j: (l, 0, 0)),
        ],
        out_specs=[
            pl.BlockSpec((tm, tn), lambda i, j: (i, j)),
            pl.BlockSpec((tm, N_SMALL), lambda i, j: (i, 0)),
            pl.BlockSpec((1, BRANCH_W, tm), lambda i, j: (i // nS, 0, i % nS)),
        ],
        out_shape=[
            jax.ShapeDtypeStruct((T, N_MAIN), BF16),
            jax.ShapeDtypeStruct((T, N_SMALL), F32),
            jax.ShapeDtypeStruct((T // S, BRANCH_W, S), BF16),
        ],
        scratch_shapes=[pltpu.VMEM((tm, D_MODEL), BF16)],
        compiler_params=_cparams(("parallel", "arbitrary")),
        name="proj_in",
    )(x2, gain, w_main, w_small, w_fvt)


def _cumsum_kernel(s_ref, b_ref, al_ref, c_ref, ct_ref, gt_ref):
    S = s_ref.shape[1]
    x = s_ref[0] + b_ref[...]
    lane = lax.broadcasted_iota(jnp.int32, (S, LANES), 1)
    gt_ref[0] = jnp.where(lane < S_GA, _sigmoid(x), -jnp.exp(al_ref[...]) * _softplus(x))
    logf = -_softplus(-x) * LOG2E
    r = lax.broadcasted_iota(jnp.int32, (LANES, LANES), 0)
    c = lax.broadcasted_iota(jnp.int32, (LANES, LANES), 1)
    tri = (r >= c).astype(F32)
    carry = jnp.zeros((1, LANES), F32)
    for blk in range(S // LANES):
        seg = logf[blk * LANES:(blk + 1) * LANES]
        cs = jnp.dot(tri, seg, precision=HIGHEST, preferred_element_type=F32) + carry
        c_ref[0, blk * LANES:(blk + 1) * LANES, :] = cs
        carry = cs[LANES - 1:LANES, :]
    ct_ref[0] = c_ref[0].T[:HEADS]


def _gates(small3, bias_row, alog_row):
    B, S, _ = small3.shape
    return pl.pallas_call(
        _cumsum_kernel,
        grid=(B,),
        in_specs=[
            pl.BlockSpec((1, S, LANES), lambda b: (b, 0, 2)),
            pl.BlockSpec((1, LANES), lambda b: (0, 0)),
            pl.BlockSpec((1, LANES), lambda b: (0, 0)),
        ],
        out_specs=[
            pl.BlockSpec((1, S, LANES), lambda b: (b, 0, 0)),
            pl.BlockSpec((1, HEADS, S), lambda b: (b, 0, 0)),
            pl.BlockSpec((1, S, LANES), lambda b: (b, 0, 0)),
        ],
        out_shape=[
            jax.ShapeDtypeStruct((B, S, LANES), F32),
            jax.ShapeDtypeStruct((B, HEADS, S), F32),
            jax.ShapeDtypeStruct((B, S, LANES), F32),
        ],
        compiler_params=_cparams(("parallel",)),
        name="gates",
    )(small3, bias_row, alog_row)


def _attn_kernel(*refs, dk, scale, decay, tq):
    if decay:
        q_ref, k_ref, vt_ref, cq_ref, ck_ref, o_ref = refs
    else:
        q_ref, k_ref, vt_ref, o_ref = refs
    i = pl.program_id(1)
    krow = lax.broadcasted_iota(jnp.int32, (tq, tq), 0)
    qcol = lax.broadcasted_iota(jnp.int32, (tq, tq), 1)
    causal = krow <= qcol
    c2 = scale * LOG2E

    def group_step(hs, qs, cqs, j, carry, masked):
        off = pl.multiple_of(j * tq, tq)
        ts = []
        for h, q in zip(hs, qs):
            k = k_ref[0, pl.ds(off, tq), h * dk:(h + 1) * dk]
            t = lax.dot_general(k, q, (((1,), (1,)), ((), ())), preferred_element_type=F32) * c2
            if decay:
                t = t - ck_ref[0, pl.ds(off, tq), h:h + 1]
            if masked:
                t = jnp.where(causal, t, -jnp.inf)
            ts.append(t)
        ps = []
        for t, cq, (m, l, acc) in zip(ts, cqs, carry):
            mt = jnp.max(t, axis=0, keepdims=True)
            m_new = jnp.maximum(m, mt + cq if decay else mt)
            p = jnp.exp2(t - (m_new - cq if decay else m_new))
            alpha = jnp.exp2(m - m_new)
            ps.append((m_new, alpha, alpha * l + jnp.sum(p, axis=0, keepdims=True), p.astype(BF16)))
        out = []
        for h, (m_new, alpha, l, p), (_, _, acc) in zip(hs, ps, carry):
            vt = vt_ref[0, h * HEAD_DIM:(h + 1) * HEAD_DIM, pl.ds(off, tq)]
            out.append((m_new, l, alpha * acc + jnp.dot(vt, p, preferred_element_type=F32)))
        return tuple(out)

    for h0 in range(0, HEADS, ATTN_GROUP):
        hs = tuple(range(h0, h0 + ATTN_GROUP))
        qs = [q_ref[0, :, h * dk:(h + 1) * dk] for h in hs]
        cqs = [cq_ref[0, h:h + 1, :] if decay else None for h in hs]

        def step(j, carry, masked, hs=hs, qs=qs, cqs=cqs):
            return group_step(hs, qs, cqs, j, carry, masked)

        init = tuple((jnp.full((1, tq), -jnp.inf, F32), jnp.zeros((1, tq), F32),
                      jnp.zeros((HEAD_DIM, tq), F32)) for _ in hs)
        carry = lax.fori_loop(0, i, functools.partial(step, masked=False), init)
        for h, (_, l, acc) in zip(hs, step(i, carry, masked=True)):
            o_ref[0, :, h * HEAD_DIM:(h + 1) * HEAD_DIM] = (acc / l).T.astype(BF16)


def _attention(q_arr, k_arr, vt_arr, q_blk, k_blk, dk, scale, cum=None, tq=512):
    B, S, _ = q_arr.shape
    decay = cum is not None
    in_specs = [
        pl.BlockSpec((1, tq, HEADS * dk), lambda b, i: (b, i, q_blk)),
        pl.BlockSpec((1, S, HEADS * dk), lambda b, i: (b, 0, k_blk)),
        pl.BlockSpec((1, BRANCH_W, S), lambda b, i: (b, 0, 0)),
    ]
    args = [q_arr, k_arr, vt_arr]
    if decay:
        in_specs += [
            pl.BlockSpec((1, HEADS, tq), lambda b, i: (b, 0, i)),
            pl.BlockSpec((1, S, LANES), lambda b, i: (b, 0, 0)),
        ]
        args += [cum[1], cum[0]]
    return pl.pallas_call(
        functools.partial(_attn_kernel, dk=dk, scale=scale, decay=decay, tq=tq),
        grid=(B, S // tq),
        in_specs=in_specs,
        out_specs=pl.BlockSpec((1, tq, BRANCH_W), lambda b, i: (b, i, 0)),
        out_shape=jax.ShapeDtypeStruct((B, S, BRANCH_W), BF16),
        compiler_params=_cparams(("parallel", "arbitrary")),
        name="fox_attn" if decay else "mla_attn",
    )(*args)


def _mla_prep_kernel(cq_ref, ckv_ref, sm_ref, qn_ref, kvn_ref, wqa_ref, wqb_ref, wk_ref, wv_ref,
                     cos_ref, sin_ref, q_ref, k_ref, v_ref):
    cos = cos_ref[...]
    sin = sin_ref[...]
    qn = _rms(cq_ref[...].astype(F32), qn_ref[...]).astype(BF16)
    a = jnp.dot(qn, wqa_ref[...], preferred_element_type=F32)
    bsw = jnp.dot(qn, wqb_ref[...], preferred_element_type=F32)
    kvn = _rms(ckv_ref[...].astype(F32), kvn_ref[...]).astype(BF16)
    kn = jnp.dot(kvn, wk_ref[...], preferred_element_type=F32)
    v_ref[0] = lax.dot_general(wv_ref[...], kvn, (((1,), (1,)), ((), ())),
                               preferred_element_type=F32).astype(BF16)
    sm = sm_ref[...]
    kpe = (sm[:, 0:LANES] * cos + sm[:, LANES:2 * LANES] * sin).astype(BF16)
    for h in range(HEADS):
        lo = h * MLA_QK
        q_ref[:, lo:lo + LANES] = a[:, lo:lo + LANES].astype(BF16)
        q_ref[:, lo + LANES:lo + 2 * LANES] = (
            a[:, lo + LANES:lo + 2 * LANES] * cos + bsw[:, h * LANES:(h + 1) * LANES] * sin
        ).astype(BF16)
        k_ref[:, lo:lo + LANES] = kn[:, h * LANES:(h + 1) * LANES].astype(BF16)
        k_ref[:, lo + LANES:lo + 2 * LANES] = kpe


def _mla_prep(p_main, p_small, qn, kvn, wqa, wqb, wk, wv, cos_t, sin_t, S, tm=512):
    T = p_main.shape[0]
    nS = S // tm
    const = lambda i: (0, 0)
    return pl.pallas_call(
        _mla_prep_kernel,
        grid=(T // tm,),
        in_specs=[
            pl.BlockSpec((tm, MLA_RANK), lambda i: (i, C_CQ // MLA_RANK)),
            pl.BlockSpec((tm, MLA_RANK), lambda i: (i, C_CKV // MLA_RANK)),
            pl.BlockSpec((tm, N_SMALL), lambda i: (i, 0)),
            pl.BlockSpec((1, MLA_RANK), const),
            pl.BlockSpec((1, MLA_RANK), const),
            pl.BlockSpec((MLA_RANK, HEADS * MLA_QK), const),
            pl.BlockSpec((MLA_RANK, HEADS * LANES), const),
            pl.BlockSpec((MLA_RANK, BRANCH_W), const),
            pl.BlockSpec((BRANCH_W, MLA_RANK), const),
            pl.BlockSpec((tm, LANES), lambda i: (i % nS, 0)),
            pl.BlockSpec((tm, LANES), lambda i: (i % nS, 0)),
        ],
        out_specs=[
            pl.BlockSpec((tm, HEADS * MLA_QK), lambda i: (i, 0)),
            pl.BlockSpec((tm, HEADS * MLA_QK), lambda i: (i, 0)),
            pl.BlockSpec((1, BRANCH_W, tm), lambda i: (i // nS, 0, i % nS)),
        ],
        out_shape=[
            jax.ShapeDtypeStruct((T, HEADS * MLA_QK), BF16),
            jax.ShapeDtypeStruct((T, HEADS * MLA_QK), BF16),
            jax.ShapeDtypeStruct((T // S, BRANCH_W, S), BF16),
        ],
        compiler_params=_cparams(("parallel",)),
        name="mla_prep",
    )(p_main, p_main, p_small, qn, kvn, wqa, wqb, wk, wv, cos_t, sin_t)


def _gdn_kernel(q_ref, k_ref, v_ref, z_ref, gt_ref, cwq_ref, cwk_ref, cwv_ref, on_ref,
                mk_ref, o_ref, qn_s, kn_s, vn_s, g_s, b_s, p_s, n_s, qp_s, oo_s, st_s, cd_s, xp_s):
    h0 = pl.program_id(1) * GDN_HP
    S = q_ref.shape[1]
    C = GDN_C
    NC = S // C
    lane = lax.broadcasted_iota(jnp.int32, (S, LANES), 1)

    xp_s[0:8, :] = jnp.zeros((8, LANES), F32)

    def conv_silu(x_ref, cw_ref, hh):
        cols = pl.ds(pl.multiple_of(hh * LANES, LANES), LANES)
        w = cw_ref[:, cols] * 0.5
        xp_s[8:, :] = x_ref[0, :, cols].astype(F32)
        hy = None
        for sft in range(GDN_CONV):
            t = xp_s[8 - sft:8 - sft + S, :] * w[GDN_CONV - 1 - sft:GDN_CONV - sft]
            hy = t if hy is None else hy + t
        return hy + hy * jnp.tanh(hy)

    def l2n(x):
        return x * lax.rsqrt(jnp.sum(x * x, axis=-1, keepdims=True) + EPS)

    def prologue(hh, _):
        rows = pl.ds(pl.multiple_of(hh * S, S), S)
        qn_s[rows, :] = l2n(conv_silu(q_ref, cwq_ref, hh)) * (HEAD_DIM ** -0.5)
        kn_s[rows, :] = l2n(conv_silu(k_ref, cwk_ref, hh))
        vn_s[rows, :] = conv_silu(v_ref, cwv_ref, hh)
        gates = gt_ref[0]
        pick = lambda col: jnp.broadcast_to(
            jnp.sum(jnp.where(lane == col + h0 + hh, gates, 0.0), axis=-1, keepdims=True),
            (S, LANES))
        b_s[rows, :] = pick(S_GB)
        g_s[rows, :] = pick(S_GA)
        return 0

    lax.fori_loop(0, GDN_HP, prologue, 0)

    tri = mk_ref[0]
    eye = mk_ref[1]
    incl = tri > 0.5
    strict = tri - eye

    nt = (((1,), (1,)), ((), ()))
    dot = functools.partial(jnp.dot, preferred_element_type=F32)

    def local(n, _):
        G = range(GDN_GROUP)
        sls = [pl.ds(pl.multiple_of((n * GDN_GROUP + g) * C, C), C) for g in G]
        q = [qn_s[sl, :] for sl in sls]
        k = [kn_s[sl, :] for sl in sls]
        beta = [b_s[sl, :] for sl in sls]
        gc = [jnp.dot(tri, g_s[sl, :], precision=HIGHEST, preferred_element_type=F32) for sl in sls]
        decay = [jnp.exp(jnp.where(incl, x - x.T, -jnp.inf)) for x in gc]
        kb = [k[g] * beta[g] for g in G]
        kbf = [x.astype(BF16) for x in k]
        a = [lax.dot_general(kb[g].astype(BF16), kbf[g], nt, preferred_element_type=F32)
             * decay[g] * strict for g in G]
        intra = [(lax.dot_general(q[g].astype(BF16), kbf[g], nt, preferred_element_type=F32)
                  * decay[g]).astype(BF16) for g in G]
        t = [eye - a[g] * mk_ref[2] for g in G]
        for lvl in range(1, int(math.log2(C))):
            tb = [x.astype(BF16) for x in t]
            tc = [dot(tb[g], (a[g] * mk_ref[2 + lvl]).astype(BF16)).astype(BF16) for g in G]
            t = [t[g] - dot(tc[g], tb[g]) for g in G]
        tb = [x.astype(BF16) for x in t]
        eg = [jnp.exp(x) for x in gc]
        ub = [dot(tb[g], (vn_s[sls[g], :] * beta[g]).astype(BF16)).astype(BF16) for g in G]
        wb = [dot(tb[g], (kb[g] * eg[g]).astype(BF16)).astype(BF16) for g in G]
        glast = [x[C - 1:C, :] for x in gc]
        kdt = [(k[g] * jnp.exp(glast[g] - gc[g])).T.astype(BF16) for g in G]
        for g in G:
            p_s[sls[g], :] = dot(kdt[g], wb[g]).astype(BF16)
        for g in G:
            n_s[sls[g], :] = dot(kdt[g], ub[g])
        for g in G:
            qp_s[sls[g], :] = (q[g] * eg[g] - dot(intra[g], wb[g])).astype(BF16)
        for g in G:
            oo_s[sls[g], :] = dot(intra[g], ub[g])
            cd_s[n * GDN_GROUP + g] = jnp.broadcast_to(jnp.exp(glast[g]), (8, LANES))
        return 0

    lax.fori_loop(0, GDN_HP * NC // GDN_GROUP, local, 0)

    states = [jnp.zeros((HEAD_DIM, HEAD_DIM), F32) for _ in range(GDN_HP)]
    for n in range(NC):
        rows = [slice((hh * NC + n) * C, (hh * NC + n + 1) * C) for hh in range(GDN_HP)]
        sbs = [s.astype(BF16) for s in states]
        for r, sb in zip(rows, sbs):
            st_s[r, :] = sb
        ps = [dot(p_s[r, :], sb) for r, sb in zip(rows, sbs)]
        states = [states[hh] * cd_s[hh * NC + n][0:1, :] + n_s[rows[hh], :] - ps[hh]
                  for hh in range(GDN_HP)]

    for hh in range(GDN_HP):
        def emit(n, _, hh=hh):
            G = range(GDN_GROUP)
            offs = [pl.multiple_of((n * GDN_GROUP + g) * C, C) for g in G]
            sls = [pl.ds(hh * S + off, C) for off in offs]
            outs = [dot(qp_s[sl, :], st_s[sl, :]) + oo_s[sl, :] for sl in sls]
            for off, out in zip(offs, outs):
                z = z_ref[0, pl.ds(off, C), hh * LANES:(hh + 1) * LANES].astype(F32)
                o_ref[0, pl.ds(off, C), hh * LANES:(hh + 1) * LANES] = (
                    _rms(out, on_ref[...]) * (z * _sigmoid(z))).astype(BF16)
            return 0

        lax.fori_loop(0, NC // GDN_GROUP, emit, 0)


def _gdn_masks():
    C = GDN_C
    i = np.arange(C)[:, None]
    j = np.arange(C)[None, :]
    mats = [(i >= j), (i == j)]
    s = 1
    while s < C:
        mats.append(((i // (2 * s)) == (j // (2 * s))) & ((i // s) % 2 == 1) & ((j // s) % 2 == 0))
        s *= 2
    return jnp.asarray(np.stack(mats).astype(np.float32))


def _gdn(p_main3, gates, conv_w, out_norm, masks):
    B, S, _ = p_main3.shape
    assert S % (GDN_C * GDN_GROUP) == 0, S
    nm = masks.shape[0]
    W = GDN_HP * LANES
    cb = lambda base: (lambda b, h: (b, 0, base // W + h))
    cw = lambda base: (lambda b, h: (0, base // W + h))
    seq = lambda dt=F32: pltpu.VMEM((GDN_HP * S, LANES), dt)
    return pl.pallas_call(
        _gdn_kernel,
        grid=(B, HEADS // GDN_HP),
        in_specs=[
            pl.BlockSpec((1, S, W), cb(C_GQ)),
            pl.BlockSpec((1, S, W), cb(C_GK)),
            pl.BlockSpec((1, S, W), cb(C_GV)),
            pl.BlockSpec((1, S, W), cb(C_GZ)),
            pl.BlockSpec((1, S, LANES), lambda b, h: (b, 0, 0)),
            pl.BlockSpec((GDN_CONV, W), cw(0)),
            pl.BlockSpec((GDN_CONV, W), cw(BRANCH_W)),
            pl.BlockSpec((GDN_CONV, W), cw(2 * BRANCH_W)),
            pl.BlockSpec((1, LANES), lambda b, h: (0, 0)),
            pl.BlockSpec((nm, GDN_C, GDN_C), lambda b, h: (0, 0, 0)),
        ],
        out_specs=pl.BlockSpec((1, S, W), lambda b, h: (b, 0, h)),
        out_shape=jax.ShapeDtypeStruct((B, S, BRANCH_W), BF16),
        scratch_shapes=[seq() for _ in range(5)]
        + [seq(BF16), seq(), seq(BF16), seq(), seq(BF16)]
        + [pltpu.VMEM((GDN_HP * S // GDN_C, 8, LANES), F32), pltpu.VMEM((S + 8, LANES), F32)],
        compiler_params=_cparams(("parallel", "arbitrary")),
        name="gdn",
    )(p_main3, p_main3, p_main3, p_main3, gates, conv_w, conv_w, conv_w, out_norm, masks)


def _merge_kernel(of_ref, om_ref, og_ref, wb_ref, g0_ref, g1_ref, g2_ref, o_ref):
    acc = None
    for o, g, br in ((of_ref, g0_ref, 0), (om_ref, g1_ref, 1), (og_ref, g2_ref, 2)):
        y = _sigmoid(g[...].astype(F32)) * jnp.dot(o[...], wb_ref[br], preferred_element_type=F32)
        acc = y if acc is None else acc + y
    o_ref[...] = acc.astype(BF16)


def _merge(o_fox, o_mla, o_gdn, w_branch, l, p_main, tm=1024, tn=512):
    T = o_fox.shape[0]
    osp = pl.BlockSpec((tm, BRANCH_W), lambda i, j: (i, 0))
    gsp = lambda br: pl.BlockSpec((tm, tn), lambda i, j: (i, (C_GATE + br * D_MODEL) // tn + j))
    return pl.pallas_call(
        _merge_kernel,
        grid=(T // tm, D_MODEL // tn),
        in_specs=[osp, osp, osp,
                  pl.BlockSpec((None, 3, BRANCH_W, tn), lambda i, j: (l, 0, 0, j)),
                  gsp(0), gsp(1), gsp(2)],
        out_specs=pl.BlockSpec((tm, tn), lambda i, j: (i, j)),
        out_shape=jax.ShapeDtypeStruct((T, D_MODEL), BF16),
        compiler_params=_cparams(("parallel", "arbitrary")),
        name="merge",
    )(o_fox, o_mla, o_gdn, w_branch, p_main, p_main, p_main)


def _outproj_kernel(m_ref, w_ref, x_ref, o_ref):
    o_ref[...] = x_ref[...] + jnp.dot(m_ref[...], w_ref[...], preferred_element_type=F32)


def _outproj(merged, w_out, l, x2, tm=1024, tn=1024):
    T = merged.shape[0]
    return pl.pallas_call(
        _outproj_kernel,
        grid=(T // tm, D_MODEL // tn),
        in_specs=[
            pl.BlockSpec((tm, D_MODEL), lambda i, j: (i, 0)),
            pl.BlockSpec((None, D_MODEL, tn), lambda i, j: (l, 0, j)),
            pl.BlockSpec((tm, tn), lambda i, j: (i, j)),
        ],
        out_specs=pl.BlockSpec((tm, tn), lambda i, j: (i, j)),
        out_shape=jax.ShapeDtypeStruct((T, D_MODEL), F32),
        compiler_params=_cparams(("parallel", "arbitrary")),
        name="outproj",
    )(merged, w_out, x2)


def _ffn_kernel(x_ref, g_ref, wu_ref, wd_ref, fg_ref, o_ref, h_ref, *, final):
    f = pl.program_id(1)

    @pl.when(f == 0)
    def _():
        x = x_ref[...]
        h_ref[...] = _rms(x, g_ref[...]).astype(BF16)
        o_ref[...] = x

    hid = jnp.maximum(jnp.dot(h_ref[...], wu_ref[...], preferred_element_type=F32), 0.0)
    o_ref[...] += jnp.dot((hid * hid).astype(BF16), wd_ref[...], preferred_element_type=F32)

    if final:
        @pl.when(f == pl.num_programs(1) - 1)
        def _():
            o_ref[...] = _rms(o_ref[...], fg_ref[...])


def _ffn(x2, gain, w_up, w_down, l, final_gain, final, tm=1024, tf=1024):
    T = x2.shape[0]
    return pl.pallas_call(
        functools.partial(_ffn_kernel, final=final),
        grid=(T // tm, D_FF // tf),
        in_specs=[
            pl.BlockSpec((tm, D_MODEL), lambda i, f: (i, 0), pipeline_mode=pl.Buffered(1)),
            pl.BlockSpec((1, D_MODEL), lambda i, f: (0, 0)),
            pl.BlockSpec((None, D_MODEL, tf), lambda i, f: (l, 0, f)),
            pl.BlockSpec((None, tf, D_MODEL), lambda i, f: (l, f, 0)),
            pl.BlockSpec((1, D_MODEL), lambda i, f: (0, 0)),
        ],
        out_specs=pl.BlockSpec((tm, D_MODEL), lambda i, f: (i, 0)),
        out_shape=jax.ShapeDtypeStruct((T, D_MODEL), F32),
        scratch_shapes=[pltpu.VMEM((tm, D_MODEL), BF16)],
        compiler_params=_cparams(("parallel", "arbitrary")),
        name="ffn",
    )(x2, gain, w_up, w_down, final_gain)


def _rope_tables(S):
    inv_freq = ROPE_THETA ** (-jnp.arange(0, MLA_ROPE, 2, dtype=F32) / MLA_ROPE)
    ang = jnp.arange(S, dtype=F32)[:, None] * inv_freq[None, :]
    cos, sin = jnp.cos(ang), jnp.sin(ang)
    zeros = jnp.zeros((S, LANES - MLA_ROPE), F32)
    return (jnp.concatenate([cos, cos, zeros], axis=1),
            jnp.concatenate([-sin, sin, zeros], axis=1))


def _swap_halves(w):
    half = MLA_ROPE // 2
    return jnp.concatenate([w[..., half:], w[..., :half]], axis=-1)


def _split_w_in(w):
    sizes = (BRANCH_W, BRANCH_W, BRANCH_W, HEADS, MLA_RANK, MLA_RANK, MLA_ROPE,
             3 * BRANCH_W, BRANCH_W, HEADS, HEADS, 3 * D_MODEL)
    offs = np.cumsum((0,) + sizes)
    seg = lambda n: w[..., offs[n]:offs[n + 1]]
    fq, fk, fv, ff, cq, ckv, kpe, gqkv, gz, gb, ga, gate = (seg(n) for n in range(12))
    main = jnp.concatenate([fq, fk, cq, ckv, gqkv, gz, gate], axis=-1).astype(BF16)
    z = lambda n: jnp.zeros(w.shape[:-1] + (n,), F32)
    small = jnp.concatenate(
        [kpe, z(LANES - MLA_ROPE), _swap_halves(kpe), z(LANES - MLA_ROPE),
         ff, gb, ga, z(LANES - 3 * HEADS)], axis=-1).astype(BF16)
    return main, small, jnp.swapaxes(fv, -1, -2).astype(BF16)


def _split_mla(w_uq, w_ukv):
    nope, pe = w_uq[..., :HEAD_DIM], w_uq[..., HEAD_DIM:]
    zpad = jnp.zeros((MLA_RANK, HEADS, LANES - MLA_ROPE), F32)
    wqa = jnp.concatenate([nope, pe, zpad], axis=-1).reshape(MLA_RANK, HEADS * MLA_QK)
    wqb = jnp.concatenate([_swap_halves(pe), zpad], axis=-1).reshape(MLA_RANK, HEADS * LANES)
    wk = w_ukv[..., :HEAD_DIM].reshape(MLA_RANK, BRANCH_W)
    wvt = w_ukv[..., HEAD_DIM:].reshape(MLA_RANK, BRANCH_W).T
    return tuple(a.astype(BF16) for a in (wqa, wqb, wk, wvt))


@jax.jit
def _forward(x, attn_norm, w_in, fox_fgate_bias, mla_q_norm, mla_kv_norm, w_mla_uq, w_mla_ukv,
             gdn_conv, gdn_a_log, gdn_dt_bias, gdn_out_norm, w_branch, w_out, mlp_norm,
             w_up, w_down, final_norm):
    B, S, D = x.shape
    T = B * S
    depth = w_in.shape[0]
    cos_t, sin_t = _rope_tables(S)
    masks = _gdn_masks()
    row = lambda v: v.reshape(1, -1).astype(F32)
    x2 = x.reshape(T, D)
    w_main, w_small, w_fvt = _split_w_in(w_in)
    wb_all, wo_all = w_branch.astype(BF16), w_out.astype(BF16)
    wu_all, wd_all = w_up.astype(BF16), w_down.astype(BF16)
    for l in range(depth):
        p_main, p_small, fv_t = _proj_in(x2, row(attn_norm[l]), w_main, w_small, w_fvt, l, S)
        pm3 = p_main.reshape(B, S, N_MAIN)
        ps3 = p_small.reshape(B, S, N_SMALL)

        bias_row = (jnp.zeros((1, LANES), F32).at[0, S_FF:S_FF + HEADS].set(fox_fgate_bias[l])
                    .at[0, S_GA:S_GA + HEADS].set(gdn_dt_bias[l]))
        alog_row = jnp.zeros((1, LANES), F32).at[0, S_GA:S_GA + HEADS].set(gdn_a_log[l])
        cum_c, cum_r, gdn_gates = _gates(ps3, bias_row, alog_row)
        cum = (cum_c, cum_r)
        o_fox = _attention(pm3, pm3, fv_t, C_FQ // BRANCH_W, C_FK // BRANCH_W,
                           HEAD_DIM, HEAD_DIM ** -0.5, cum=cum)

        wqa, wqb, wk, wvt = _split_mla(w_mla_uq[l], w_mla_ukv[l])
        qf, kf, mv_t = _mla_prep(p_main, p_small, row(mla_q_norm[l]), row(mla_kv_norm[l]),
                                 wqa, wqb, wk, wvt, cos_t, sin_t, S)
        shp = lambda a: a.reshape(B, S, a.shape[-1])
        o_mla = _attention(shp(qf), shp(kf), mv_t, 0, 0, MLA_QK, (HEAD_DIM + MLA_ROPE) ** -0.5)

        o_gdn = _gdn(pm3, gdn_gates, gdn_conv[l], row(gdn_out_norm[l]), masks)

        merged = _merge(o_fox.reshape(T, BRANCH_W), o_mla.reshape(T, BRANCH_W),
                        o_gdn.reshape(T, BRANCH_W), wb_all, l, p_main)
        x2 = _outproj(merged, wo_all, l, x2)
        x2 = _ffn(x2, row(mlp_norm[l]), wu_all, wd_all, l, row(final_norm),
                  final=(l == depth - 1))
    return x2.reshape(B, S, D)


def kernel(x, attn_norm, w_in, fox_fgate_bias, mla_q_norm, mla_kv_norm, w_mla_uq, w_mla_ukv,
           gdn_conv, gdn_a_log, gdn_dt_bias, gdn_out_norm, w_branch, w_out, mlp_norm,
           w_up, w_down, final_norm):
    return _forward(x, attn_norm, w_in, fox_fgate_bias, mla_q_norm, mla_kv_norm, w_mla_uq,
                    w_mla_ukv, gdn_conv, gdn_a_log, gdn_dt_bias, gdn_out_norm, w_branch, w_out,
                    mlp_norm, w_up, w_down, final_norm)
```

```python
import functools
import math

import numpy as np
import jax
import jax.numpy as jnp
from jax import lax
from jax.experimental import pallas as pl
from jax.experimental.pallas import tpu as pltpu

F32 = jnp.float32
BF16 = jnp.bfloat16
HIGHEST = lax.Precision.HIGHEST

D_MODEL = 2048
HEAD_DIM = 128
HEADS = 8
BRANCH_W = HEADS * HEAD_DIM
MLA_RANK = 512
MLA_ROPE = 64
MLA_QK = 256
ROPE_THETA = 10000.0
GDN_CONV = 4
D_FF = 4 * D_MODEL
EPS = 1e-6
LOG2E = math.log2(math.e)

LANES = 128
VMEM_LIMIT = 52 * 1024 * 1024

C_FQ, C_FK = 0, 1024
C_CQ, C_CKV = 2048, 2560
C_GQ, C_GK, C_GV, C_GZ = 3072, 4096, 5120, 6144
C_GATE = 7168
N_MAIN = C_GATE + 3 * D_MODEL
N_SMALL = 3 * LANES
S_FF, S_GB, S_GA = 0, 8, 16

GDN_C = 128
ATTN_GROUP = 8
GDN_HP = 2
GDN_GROUP = 8


def _cparams(sem):
    return pltpu.CompilerParams(dimension_semantics=sem, vmem_limit_bytes=VMEM_LIMIT)


def _sigmoid(x):
    return 0.5 * jnp.tanh(0.5 * x) + 0.5


def _softplus(x):
    return jnp.maximum(x, 0.0) + jnp.log(1.0 + jnp.exp(-jnp.abs(x)))


def _rms(x, gain):
    return x * lax.rsqrt(jnp.mean(x * x, axis=-1, keepdims=True) + EPS) * gain


def _proj_kernel(x_ref, g_ref, w_ref, ws_ref, wv_ref, o_ref, os_ref, vt_ref, u_ref):
    @pl.when(pl.program_id(1) == 0)
    def _():
        u = _rms(x_ref[...], g_ref[...]).astype(BF16)
        u_ref[...] = u
        os_ref[...] = jnp.dot(u, ws_ref[...], preferred_element_type=F32)
        vt_ref[0] = jnp.dot(u, wv_ref[...], preferred_element_type=F32).T.astype(BF16)

    o_ref[...] = jnp.dot(u_ref[...], w_ref[...], preferred_element_type=F32).astype(BF16)


def _proj_in(x2, gain, w_main, w_small, w_fvt, l, S, tm=1024, tn=1024):
    T = x2.shape[0]
    nS = S // tm
    return pl.pallas_call(
        _proj_kernel,
        grid=(T // tm, N_MAIN // tn),
        in_specs=[
            pl.BlockSpec((tm, D_MODEL), lambda i, j: (i, 0), pipeline_mode=pl.Buffered(1)),
            pl.BlockSpec((1, D_MODEL), lambda i, j: (0, 0)),
            pl.BlockSpec((None, D_MODEL, tn), lambda i, j: (l, 0, j)),
            pl.BlockSpec((None, D_MODEL, N_SMALL), lambda i, j: (l, 0, 0)),
            pl.BlockSpec((None, D_MODEL, BRANCH_W), lambda i, j: (l, 0, 0)),
        ],
        out_specs=[
            pl.BlockSpec((tm, tn), lambda i, j: (i, j)),
            pl.BlockSpec((tm, N_SMALL), lambda i, j: (i, 0)),
            pl.BlockSpec((1, BRANCH_W, tm), lambda i, j: (i // nS, 0, i % nS)),
        ],
        out_shape=[
            jax.ShapeDtypeStruct((T, N_MAIN), BF16),
            jax.ShapeDtypeStruct((T, N_SMALL), F32),
            jax.ShapeDtypeStruct((T // S, BRANCH_W, S), BF16),
        ],
        scratch_shapes=[pltpu.VMEM((tm, D_MODEL), BF16)],
        compiler_params=_cparams(("parallel", "arbitrary")),
        name="proj_in",
    )(x2, gain, w_main, w_small, w_fvt)


def _cumsum_kernel(s_ref, b_ref, al_ref, c_ref, ct_ref, gt_ref):
    S = s_ref.shape[1]
    x = s_ref[0] + b_ref[...]
    lane = lax.broadcasted_iota(jnp.int32, (S, LANES), 1)
    gt_ref[0] = jnp.where(lane < S_GA, _sigmoid(x), -jnp.exp(al_ref[...]) * _softplus(x))
    logf = -_softplus(-x) * LOG2E
    r = lax.broadcasted_iota(jnp.int32, (LANES, LANES), 0)
    c = lax.broadcasted_iota(jnp.int32, (LANES, LANES), 1)
    tri = (r >= c).astype(F32)
    carry = jnp.zeros((1, LANES), F32)
    for blk in range(S // LANES):
        seg = logf[blk * LANES:(blk + 1) * LANES]
        cs = jnp.dot(tri, seg, precision=HIGHEST, preferred_element_type=F32) + carry
        c_ref[0, blk * LANES:(blk + 1) * LANES, :] = cs
        carry = cs[LANES - 1:LANES, :]
    ct_ref[0] = c_ref[0].T[:HEADS]


def _gates(small3, bias_row, alog_row):
    B, S, _ = small3.shape
    return pl.pallas_call(
        _cumsum_kernel,
        grid=(B,),
        in_specs=[
            pl.BlockSpec((1, S, LANES), lambda b: (b, 0, 2)),
            pl.BlockSpec((1, LANES), lambda b: (0, 0)),
            pl.BlockSpec((1, LANES), lambda b: (0, 0)),
        ],
        out_specs=[
            pl.BlockSpec((1, S, LANES), lambda b: (b, 0, 0)),
            pl.BlockSpec((1, HEADS, S), lambda b: (b, 0, 0)),
            pl.BlockSpec((1, S, LANES), lambda b: (b, 0, 0)),
        ],
        out_shape=[
            jax.ShapeDtypeStruct((B, S, LANES), F32),
            jax.ShapeDtypeStruct((B, HEADS, S), F32),
            jax.ShapeDtypeStruct((B, S, LANES), F32),
        ],
        compiler_params=_cparams(("parallel",)),
        name="gates",
    )(small3, bias_row, alog_row)


def _attn_kernel(*refs, dk, scale, decay, tq):
    if decay:
        q_ref, k_ref, vt_ref, cq_ref, ck_ref, o_ref = refs
    else:
        q_ref, k_ref, vt_ref, o_ref = refs
    i = pl.program_id(1)
    krow = lax.broadcasted_iota(jnp.int32, (tq, tq), 0)
    qcol = lax.broadcasted_iota(jnp.int32, (tq, tq), 1)
    causal = krow <= qcol
    c2 = scale * LOG2E

    def group_step(hs, qs, cqs, j, carry, masked):
        off = pl.multiple_of(j * tq, tq)
        ts = []
        for h, q in zip(hs, qs):
            k = k_ref[0, pl.ds(off, tq), h * dk:(h + 1) * dk]
            t = lax.dot_general(k, q, (((1,), (1,)), ((), ())), preferred_element_type=F32) * c2
            if decay:
                t = t - ck_ref[0, pl.ds(off, tq), h:h + 1]
            if masked:
                t = jnp.where(causal, t, -jnp.inf)
            ts.append(t)
        ps = []
        for t, cq, (m, l, acc) in zip(ts, cqs, carry):
            mt = jnp.max(t, axis=0, keepdims=True)
            m_new = jnp.maximum(m, mt + cq if decay else mt)
            p = jnp.exp2(t - (m_new - cq if decay else m_new))
            alpha = jnp.exp2(m - m_new)
            ps.append((m_new, alpha, alpha * l + jnp.sum(p, axis=0, keepdims=True), p.astype(BF16)))
        out = []
        for h, (m_new, alpha, l, p), (_, _, acc) in zip(hs, ps, carry):
            vt = vt_ref[0, h * HEAD_DIM:(h + 1) * HEAD_DIM, pl.ds(off, tq)]
            out.append((m_new, l, alpha * acc + jnp.dot(vt, p, preferred_element_type=F32)))
        return tuple(out)

    for h0 in range(0, HEADS, ATTN_GROUP):
        hs = tuple(range(h0, h0 + ATTN_GROUP))
        qs = [q_ref[0, :, h * dk:(h + 1) * dk] for h in hs]
        cqs = [cq_ref[0, h:h + 1, :] if decay else None for h in hs]

        def step(j, carry, masked, hs=hs, qs=qs, cqs=cqs):
            return group_step(hs, qs, cqs, j, carry, masked)

        init = tuple((jnp.full((1, tq), -jnp.inf, F32), jnp.zeros((1, tq), F32),
                      jnp.zeros((HEAD_DIM, tq), F32)) for _ in hs)
        carry = lax.fori_loop(0, i, functools.partial(step, masked=False), init)
        for h, (_, l, acc) in zip(hs, step(i, carry, masked=True)):
            o_ref[0, :, h * HEAD_DIM:(h + 1) * HEAD_DIM] = (acc / l).T.astype(BF16)


def _attention(q_arr, k_arr, vt_arr, q_blk, k_blk, dk, scale, cum=None, tq=512):
    B, S, _ = q_arr.shape
    decay = cum is not None
    in_specs = [
        pl.BlockSpec((1, tq, HEADS * dk), lambda b, i: (b, i, q_blk)),
        pl.BlockSpec((1, S, HEADS * dk), lambda b, i: (b, 0, k_blk)),
        pl.BlockSpec((1, BRANCH_W, S), lambda b, i: (b, 0, 0)),
    ]
    args = [q_arr, k_arr, vt_arr]
    if decay:
        in_specs += [
            pl.BlockSpec((1, HEADS, tq), lambda b, i: (b, 0, i)),
            pl.BlockSpec((1, S, LANES), lambda b, i: (b, 0, 0)),
        ]
        args += [cum[1], cum[0]]
    return pl.pallas_call(
        functools.partial(_attn_kernel, dk=dk, scale=scale, decay=decay, tq=tq),
        grid=(B, S // tq),
        in_specs=in_specs,
        out_specs=pl.BlockSpec((1, tq, BRANCH_W), lambda b, i: (b, i, 0)),
        out_shape=jax.ShapeDtypeStruct((B, S, BRANCH_W), BF16),
        compiler_params=_cparams(("parallel", "arbitrary")),
        name="fox_attn" if decay else "mla_attn",
    )(*args)


def _mla_prep_kernel(cq_ref, ckv_ref, sm_ref, qn_ref, kvn_ref, wqa_ref, wqb_ref, wk_ref, wv_ref,
                     cos_ref, sin_ref, q_ref, k_ref, v_ref):
    cos = cos_ref[...]
    sin = sin_ref[...]
    qn = _rms(cq_ref[...].astype(F32), qn_ref[...]).astype(BF16)
    a = jnp.dot(qn, wqa_ref[...], preferred_element_type=F32)
    bsw = jnp.dot(qn, wqb_ref[...], preferred_element_type=F32)
    kvn = _rms(ckv_ref[...].astype(F32), kvn_ref[...]).astype(BF16)
    kn = jnp.dot(kvn, wk_ref[...], preferred_element_type=F32)
    v_ref[0] = lax.dot_general(wv_ref[...], kvn, (((1,), (1,)), ((), ())),
                               preferred_element_type=F32).astype(BF16)
    sm = sm_ref[...]
    kpe = (sm[:, 0:LANES] * cos + sm[:, LANES:2 * LANES] * sin).astype(BF16)
    for h in range(HEADS):
        lo = h * MLA_QK
        q_ref[:, lo:lo + LANES] = a[:, lo:lo + LANES].astype(BF16)
        q_ref[:, lo + LANES:lo + 2 * LANES] = (
            a[:, lo + LANES:lo + 2 * LANES] * cos + bsw[:, h * LANES:(h + 1) * LANES] * sin
        ).astype(BF16)
        k_ref[:, lo:lo + LANES] = kn[:, h * LANES:(h + 1) * LANES].astype(BF16)
        k_ref[:, lo + LANES:lo + 2 * LANES] = kpe


def _mla_prep(p_main, p_small, qn, kvn, wqa, wqb, wk, wv, cos_t, sin_t, S, tm=512):
    T = p_main.shape[0]
    nS = S // tm
    const = lambda i: (0, 0)
    return pl.pallas_call(
        _mla_prep_kernel,
        grid=(T // tm,),
        in_specs=[
            pl.BlockSpec((tm, MLA_RANK), lambda i: (i, C_CQ // MLA_RANK)),
            pl.BlockSpec((tm, MLA_RANK), lambda i: (i, C_CKV // MLA_RANK)),
            pl.BlockSpec((tm, N_SMALL), lambda i: (i, 0)),
            pl.BlockSpec((1, MLA_RANK), const),
            pl.BlockSpec((1, MLA_RANK), const),
            pl.BlockSpec((MLA_RANK, HEADS * MLA_QK), const),
            pl.BlockSpec((MLA_RANK, HEADS * LANES), const),
            pl.BlockSpec((MLA_RANK, BRANCH_W), const),
            pl.BlockSpec((BRANCH_W, MLA_RANK), const),
            pl.BlockSpec((tm, LANES), lambda i: (i % nS, 0)),
            pl.BlockSpec((tm, LANES), lambda i: (i % nS, 0)),
        ],
        out_specs=[
            pl.BlockSpec((tm, HEADS * MLA_QK), lambda i: (i, 0)),
            pl.BlockSpec((tm, HEADS * MLA_QK), lambda i: (i, 0)),
            pl.BlockSpec((1, BRANCH_W, tm), lambda i: (i // nS, 0, i % nS)),
        ],
        out_shape=[
            jax.ShapeDtypeStruct((T, HEADS * MLA_QK), BF16),
            jax.ShapeDtypeStruct((T, HEADS * MLA_QK), BF16),
            jax.ShapeDtypeStruct((T // S, BRANCH_W, S), BF16),
        ],
        compiler_params=_cparams(("parallel",)),
        name="mla_prep",
    )(p_main, p_main, p_small, qn, kvn, wqa, wqb, wk, wv, cos_t, sin_t)


def _gdn_kernel(q_ref, k_ref, v_ref, z_ref, gt_ref, cwq_ref, cwk_ref, cwv_ref, on_ref,
                mk_ref, o_ref, qn_s, kn_s, vn_s, g_s, b_s, p_s, n_s, qp_s, oo_s, st_s, cd_s, xp_s):
    h0 = pl.program_id(1) * GDN_HP
    S = q_ref.shape[1]
    C = GDN_C
    NC = S // C
    lane = lax.broadcasted_iota(jnp.int32, (S, LANES), 1)

    xp_s[0:8, :] = jnp.zeros((8, LANES), F32)

    def conv_silu(x_ref, cw_ref, hh):
        cols = pl.ds(pl.multiple_of(hh * LANES, LANES), LANES)
        w = cw_ref[:, cols] * 0.5
        xp_s[8:, :] = x_ref[0, :, cols].astype(F32)
        hy = None
        for sft in range(GDN_CONV):
            t = xp_s[8 - sft:8 - sft + S, :] * w[GDN_CONV - 1 - sft:GDN_CONV - sft]
            hy = t if hy is None else hy + t
        return hy + hy * jnp.tanh(hy)

    def l2n(x):
        return x * lax.rsqrt(jnp.sum(x * x, axis=-1, keepdims=True) + EPS)

    def prologue(hh, _):
        rows = pl.ds(pl.multiple_of(hh * S, S), S)
        qn_s[rows, :] = l2n(conv_silu(q_ref, cwq_ref, hh)) * (HEAD_DIM ** -0.5)
        kn_s[rows, :] = l2n(conv_silu(k_ref, cwk_ref, hh))
        vn_s[rows, :] = conv_silu(v_ref, cwv_ref, hh)
        gates = gt_ref[0]
        pick = lambda col: jnp.broadcast_to(
            jnp.sum(jnp.where(lane == col + h0 + hh, gates, 0.0), axis=-1, keepdims=True),
            (S, LANES))
        b_s[rows, :] = pick(S_GB)
        g_s[rows, :] = pick(S_GA)
        return 0

    lax.fori_loop(0, GDN_HP, prologue, 0)

    tri = mk_ref[0]
    eye = mk_ref[1]
    incl = tri > 0.5
    strict = tri - eye

    nt = (((1,), (1,)), ((), ()))
    dot = functools.partial(jnp.dot, preferred_element_type=F32)

    def local(n, _):
        G = range(GDN_GROUP)
        sls = [pl.ds(pl.multiple_of((n * GDN_GROUP + g) * C, C), C) for g in G]
        q = [qn_s[sl, :] for sl in sls]
        k = [kn_s[sl, :] for sl in sls]
        beta = [b_s[sl, :] for sl in sls]
        gc = [jnp.dot(tri, g_s[sl, :], precision=HIGHEST, preferred_element_type=F32) for sl in sls]
        decay = [jnp.exp(jnp.where(incl, x - x.T, -jnp.inf)) for x in gc]
        kb = [k[g] * beta[g] for g in G]
        kbf = [x.astype(BF16) for x in k]
        a = [lax.dot_general(kb[g].astype(BF16), kbf[g], nt, preferred_element_type=F32)
             * decay[g] * strict for g in G]
        intra = [(lax.dot_general(q[g].astype(BF16), kbf[g], nt, preferred_element_type=F32)
                  * decay[g]).astype(BF16) for g in G]
        t = [eye - a[g] * mk_ref[2] for g in G]
        for lvl in range(1, int(math.log2(C))):
            tb = [x.astype(BF16) for x in t]
            tc = [dot(tb[g], (a[g] * mk_ref[2 + lvl]).astype(BF16)).astype(BF16) for g in G]
            t = [t[g] - dot(tc[g], tb[g]) for g in G]
        tb = [x.astype(BF16) for x in t]
        eg = [jnp.exp(x) for x in gc]
        ub = [dot(tb[g], (vn_s[sls[g], :] * beta[g]).astype(BF16)).astype(BF16) for g in G]
        wb = [dot(tb[g], (kb[g] * eg[g]).astype(BF16)).astype(BF16) for g in G]
        glast = [x[C - 1:C, :] for x in gc]
        kdt = [(k[g] * jnp.exp(glast[g] - gc[g])).T.astype(BF16) for g in G]
        for g in G:
            p_s[sls[g], :] = dot(kdt[g], wb[g]).astype(BF16)
        for g in G:
            n_s[sls[g], :] = dot(kdt[g], ub[g])
        for g in G:
            qp_s[sls[g], :] = (q[g] * eg[g] - dot(intra[g], wb[g])).astype(BF16)
        for g in G:
            oo_s[sls[g], :] = dot(intra[g], ub[g])
            cd_s[n * GDN_GROUP + g] = jnp.broadcast_to(jnp.exp(glast[g]), (8, LANES))
        return 0

    lax.fori_loop(0, GDN_HP * NC // GDN_GROUP, local, 0)

    states = [jnp.zeros((HEAD_DIM, HEAD_DIM), F32) for _ in range(GDN_HP)]
    for n in range(NC):
        rows = [slice((hh * NC + n) * C, (hh * NC + n + 1) * C) for hh in range(GDN_HP)]
        sbs = [s.astype(BF16) for s in states]
        for r, sb in zip(rows, sbs):
            st_s[r, :] = sb
        ps = [dot(p_s[r, :], sb) for r, sb in zip(rows, sbs)]
        states = [states[hh] * cd_s[hh * NC + n][0:1, :] + n_s[rows[hh], :] - ps[hh]
                  for hh in range(GDN_HP)]

    for hh in range(GDN_HP):
        def emit(n, _, hh=hh):
            G = range(GDN_GROUP)
            offs = [pl.multiple_of((n * GDN_GROUP + g) * C, C) for g in G]
            sls = [pl.ds(hh * S + off, C) for off in offs]
            outs = [dot(qp_s[sl, :], st_s[sl, :]) + oo_s[sl, :] for sl in sls]
            for off, out in zip(offs, outs):
                z = z_ref[0, pl.ds(off, C), hh * LANES:(hh + 1) * LANES].astype(F32)
                o_ref[0, pl.ds(off, C), hh * LANES:(hh + 1) * LANES] = (
                    _rms(out, on_ref[...]) * (z * _sigmoid(z))).astype(BF16)
            return 0

        lax.fori_loop(0, NC // GDN_GROUP, emit, 0)


def _gdn_masks():
    C = GDN_C
    i = np.arange(C)[:, None]
    j = np.arange(C)[None, :]
    mats = [(i >= j), (i == j)]
    s = 1
    while s < C:
        mats.append(((i // (2 * s)) == (j // (2 * s))) & ((i // s) % 2 == 1) & ((j // s) % 2 == 0))
        s *= 2
    return jnp.asarray(np.stack(mats).astype(np.float32))


def _gdn(p_main3, gates, conv_w, out_norm, masks):
    B, S, _ = p_main3.shape
    assert S % (GDN_C * GDN_GROUP) == 0, S
    nm = masks.shape[0]
    W = GDN_HP * LANES
    cb = lambda base: (lambda b, h: (b, 0, base // W + h))
    cw = lambda base: (lambda b, h: (0, base // W + h))
    seq = lambda dt=F32: pltpu.VMEM((GDN_HP * S, LANES), dt)
    return pl.pallas_call(
        _gdn_kernel,
        grid=(B, HEADS // GDN_HP),
        in_specs=[
            pl.BlockSpec((1, S, W), cb(C_GQ)),
            pl.BlockSpec((1, S, W), cb(C_GK)),
            pl.BlockSpec((1, S, W), cb(C_GV)),
            pl.BlockSpec((1, S, W), cb(C_GZ)),
            pl.BlockSpec((1, S, LANES), lambda b, h: (b, 0, 0)),
            pl.BlockSpec((GDN_CONV, W), cw(0)),
            pl.BlockSpec((GDN_CONV, W), cw(BRANCH_W)),
            pl.BlockSpec((GDN_CONV, W), cw(2 * BRANCH_W)),
            pl.BlockSpec((1, LANES), lambda b, h: (0, 0)),
            pl.BlockSpec((nm, GDN_C, GDN_C), lambda b, h: (0, 0, 0)),
        ],
        out_specs=pl.BlockSpec((1, S, W), lambda b, h: (b, 0, h)),
        out_shape=jax.ShapeDtypeStruct((B, S, BRANCH_W), BF16),
        scratch_shapes=[seq() for _ in range(5)]
        + [seq(BF16), seq(), seq(BF16), seq(), seq(BF16)]
        + [pltpu.VMEM((GDN_HP * S // GDN_C, 8, LANES), F32), pltpu.VMEM((S + 8, LANES), F32)],
        compiler_params=_cparams(("parallel", "arbitrary")),
        name="gdn",
    )(p_main3, p_main3, p_main3, p_main3, gates, conv_w, conv_w, conv_w, out_norm, masks)


def _merge_kernel(of_ref, om_ref, og_ref, wb_ref, g0_ref, g1_ref, g2_ref, o_ref):
    acc = None
    for o, g, br in ((of_ref, g0_ref, 0), (om_ref, g1_ref, 1), (og_ref, g2_ref, 2)):
        y = _sigmoid(g[...].astype(F32)) * jnp.dot(o[...], wb_ref[br], preferred_element_type=F32)
        acc = y if acc is None else acc + y
    o_ref[...] = acc.astype(BF16)


def _merge(o_fox, o_mla, o_gdn, w_branch, l, p_main, tm=1024, tn=512):
    T = o_fox.shape[0]
    osp = pl.BlockSpec((tm, BRANCH_W), lambda i, j: (i, 0))
    gsp = lambda br: pl.BlockSpec((tm, tn), lambda i, j: (i, (C_GATE + br * D_MODEL) // tn + j))
    return pl.pallas_call(
        _merge_kernel,
        grid=(T // tm, D_MODEL // tn),
        in_specs=[osp, osp, osp,
                  pl.BlockSpec((None, 3, BRANCH_W, tn), lambda i, j: (l, 0, 0, j)),
                  gsp(0), gsp(1), gsp(2)],
        out_specs=pl.BlockSpec((tm, tn), lambda i, j: (i, j)),
        out_shape=jax.ShapeDtypeStruct((T, D_MODEL), BF16),
        compiler_params=_cparams(("parallel", "arbitrary")),
        name="merge",
    )(o_fox, o_mla, o_gdn, w_branch, p_main, p_main, p_main)


def _outproj_kernel(m_ref, w_ref, x_ref, o_ref):
    o_ref[...] = x_ref[...] + jnp.dot(m_ref[...], w_ref[...], preferred_element_type=F32)


def _outproj(merged, w_out, l, x2, tm=1024, tn=1024):
    T = merged.shape[0]
    return pl.pallas_call(
        _outproj_kernel,
        grid=(T // tm, D_MODEL // tn),
        in_specs=[
            pl.BlockSpec((tm, D_MODEL), lambda i, j: (i, 0)),
            pl.BlockSpec((None, D_MODEL, tn), lambda i, j: (l, 0, j)),
            pl.BlockSpec((tm, tn), lambda i, j: (i, j)),
        ],
        out_specs=pl.BlockSpec((tm, tn), lambda i, j: (i, j)),
        out_shape=jax.ShapeDtypeStruct((T, D_MODEL), F32),
        compiler_params=_cparams(("parallel", "arbitrary")),
        name="outproj",
    )(merged, w_out, x2)


def _ffn_kernel(x_ref, g_ref, wu_ref, wd_ref, fg_ref, o_ref, h_ref, *, final):
    f = pl.program_id(1)

    @pl.when(f == 0)
    def _():
        x = x_ref[...]
        h_ref[...] = _rms(x, g_ref[...]).astype(BF16)
        o_ref[...] = x

    hid = jnp.maximum(jnp.dot(h_ref[...], wu_ref[...], preferred_element_type=F32), 0.0)
    o_ref[...] += jnp.dot((hid * hid).astype(BF16), wd_ref[...], preferred_element_type=F32)

    if final:
        @pl.when(f == pl.num_programs(1) - 1)
        def _():
            o_ref[...] = _rms(o_ref[...], fg_ref[...])


def _ffn(x2, gain, w_up, w_down, l, final_gain, final, tm=1024, tf=1024):
    T = x2.shape[0]
    return pl.pallas_call(
        functools.partial(_ffn_kernel, final=final),
        grid=(T // tm, D_FF // tf),
        in_specs=[
            pl.BlockSpec((tm, D_MODEL), lambda i, f: (i, 0), pipeline_mode=pl.Buffered(1)),
            pl.BlockSpec((1, D_MODEL), lambda i, f: (0, 0)),
            pl.BlockSpec((None, D_MODEL, tf), lambda i, f: (l, 0, f)),
            pl.BlockSpec((None, tf, D_MODEL), lambda i, f: (l, f, 0)),
            pl.BlockSpec((1, D_MODEL), lambda i, f: (0, 0)),
        ],
        out_specs=pl.BlockSpec((tm, D_MODEL), lambda i, f: (i, 0)),
        out_shape=jax.ShapeDtypeStruct((T, D_MODEL), F32),
        scratch_shapes=[pltpu.VMEM((tm, D_MODEL), BF16)],
        compiler_params=_cparams(("parallel", "arbitrary")),
        name="ffn",
    )(x2, gain, w_up, w_down, final_gain)


def _rope_tables(S):
    inv_freq = ROPE_THETA ** (-jnp.arange(0, MLA_ROPE, 2, dtype=F32) / MLA_ROPE)
    ang = jnp.arange(S, dtype=F32)[:, None] * inv_freq[None, :]
    cos, sin = jnp.cos(ang), jnp.sin(ang)
    zeros = jnp.zeros((S, LANES - MLA_ROPE), F32)
    return (jnp.concatenate([cos, cos, zeros], axis=1),
            jnp.concatenate([-sin, sin, zeros], axis=1))


def _swap_halves(w):
    half = MLA_ROPE // 2
    return jnp.concatenate([w[..., half:], w[..., :half]], axis=-1)


def _split_w_in(w):
    sizes = (BRANCH_W, BRANCH_W, BRANCH_W, HEADS, MLA_RANK, MLA_RANK, MLA_ROPE,
             3 * BRANCH_W, BRANCH_W, HEADS, HEADS, 3 * D_MODEL)
    offs = np.cumsum((0,) + sizes)
    seg = lambda n: w[..., offs[n]:offs[n + 1]]
    fq, fk, fv, ff, cq, ckv, kpe, gqkv, gz, gb, ga, gate = (seg(n) for n in range(12))
    main = jnp.concatenate([fq, fk, cq, ckv, gqkv, gz, gate], axis=-1).astype(BF16)
    z = lambda n: jnp.zeros(w.shape[:-1] + (n,), F32)
    small = jnp.concatenate(
        [kpe, z(LANES - MLA_ROPE), _swap_halves(kpe), z(LANES - MLA_ROPE),
         ff, gb, ga, z(LANES - 3 * HEADS)], axis=-1).astype(BF16)
    return main, small, fv.astype(BF16)


def _split_mla(w_uq, w_ukv):
    nope, pe = w_uq[..., :HEAD_DIM], w_uq[..., HEAD_DIM:]
    zpad = jnp.zeros((MLA_RANK, HEADS, LANES - MLA_ROPE), F32)
    wqa = jnp.concatenate([nope, pe, zpad], axis=-1).reshape(MLA_RANK, HEADS * MLA_QK)
    wqb = jnp.concatenate([_swap_halves(pe), zpad], axis=-1).reshape(MLA_RANK, HEADS * LANES)
    wk = w_ukv[..., :HEAD_DIM].reshape(MLA_RANK, BRANCH_W)
    wvt = w_ukv[..., HEAD_DIM:].reshape(MLA_RANK, BRANCH_W).T
    return tuple(a.astype(BF16) for a in (wqa, wqb, wk, wvt))


@jax.jit
def _forward(x, attn_norm, w_in, fox_fgate_bias, mla_q_norm, mla_kv_norm, w_mla_uq, w_mla_ukv,
             gdn_conv, gdn_a_log, gdn_dt_bias, gdn_out_norm, w_branch, w_out, mlp_norm,
             w_up, w_down, final_norm):
    B, S, D = x.shape
    T = B * S
    depth = w_in.shape[0]
    cos_t, sin_t = _rope_tables(S)
    masks = _gdn_masks()
    row = lambda v: v.reshape(1, -1).astype(F32)
    x2 = x.reshape(T, D)
    w_main, w_small, w_fvt = _split_w_in(w_in)
    wb_all, wo_all = w_branch.astype(BF16), w_out.astype(BF16)
    wu_all, wd_all = w_up.astype(BF16), w_down.astype(BF16)
    for l in range(depth):
        p_main, p_small, fv_t = _proj_in(x2, row(attn_norm[l]), w_main, w_small, w_fvt, l, S)
        pm3 = p_main.reshape(B, S, N_MAIN)
        ps3 = p_small.reshape(B, S, N_SMALL)

        bias_row = (jnp.zeros((1, LANES), F32).at[0, S_FF:S_FF + HEADS].set(fox_fgate_bias[l])
                    .at[0, S_GA:S_GA + HEADS].set(gdn_dt_bias[l]))
        alog_row = jnp.zeros((1, LANES), F32).at[0, S_GA:S_GA + HEADS].set(gdn_a_log[l])
        cum_c, cum_r, gdn_gates = _gates(ps3, bias_row, alog_row)
        cum = (cum_c, cum_r)
        o_fox = _attention(pm3, pm3, fv_t, C_FQ // BRANCH_W, C_FK // BRANCH_W,
                           HEAD_DIM, HEAD_DIM ** -0.5, cum=cum)

        wqa, wqb, wk, wvt = _split_mla(w_mla_uq[l], w_mla_ukv[l])
        qf, kf, mv_t = _mla_prep(p_main, p_small, row(mla_q_norm[l]), row(mla_kv_norm[l]),
                                 wqa, wqb, wk, wvt, cos_t, sin_t, S)
        shp = lambda a: a.reshape(B, S, a.shape[-1])
        o_mla = _attention(shp(qf), shp(kf), mv_t, 0, 0, MLA_QK, (HEAD_DIM + MLA_ROPE) ** -0.5)

        o_gdn = _gdn(pm3, gdn_gates, gdn_conv[l], row(gdn_out_norm[l]), masks)

        merged = _merge(o_fox.reshape(T, BRANCH_W), o_mla.reshape(T, BRANCH_W),
                        o_gdn.reshape(T, BRANCH_W), wb_all, l, p_main)
        x2 = _outproj(merged, wo_all, l, x2)
        x2 = _ffn(x2, row(mlp_norm[l]), wu_all, wd_all, l, row(final_norm),
                  final=(l == depth - 1))
    return x2.reshape(B, S, D)


def kernel(x, attn_norm, w_in, fox_fgate_bias, mla_q_norm, mla_kv_norm, w_mla_uq, w_mla_ukv,
           gdn_conv, gdn_a_log, gdn_dt_bias, gdn_out_norm, w_branch, w_out, mlp_norm,
           w_up, w_down, final_norm):
    return _forward(x, attn_norm, w_in, fox_fgate_bias, mla_q_norm, mla_kv_norm, w_mla_uq,
                    w_mla_ukv, gdn_conv, gdn_a_log, gdn_dt_bias, gdn_out_norm, w_branch, w_out,
                    mlp_norm, w_up, w_down, final_norm)
```

```python
import functools
import math

import numpy as np
import jax
import jax.numpy as jnp
from jax import lax
from jax.experimental import pallas as pl
from jax.experimental.pallas import tpu as pltpu

F32 = jnp.float32
BF16 = jnp.bfloat16
HIGHEST = lax.Precision.HIGHEST

D_MODEL = 2048
HEAD_DIM = 128
HEADS = 8
BRANCH_W = HEADS * HEAD_DIM
MLA_RANK = 512
MLA_ROPE = 64
MLA_QK = 256
ROPE_THETA = 10000.0
GDN_CONV = 4
D_FF = 4 * D_MODEL
EPS = 1e-6
LOG2E = math.log2(math.e)

LANES = 128
VMEM_LIMIT = 52 * 1024 * 1024

C_FQ, C_FK = 0, 1024
C_CQ, C_CKV = 2048, 2560
C_GQ, C_GK, C_GV, C_GZ = 3072, 4096, 5120, 6144
C_GATE = 7168
N_MAIN = C_GATE + 3 * D_MODEL
N_SMALL = 3 * LANES
S_FF, S_GB, S_GA = 0, 8, 16

GDN_C = 128
ATTN_GROUP = 8
GDN_HP = 2
GDN_GROUP = 16


def _cparams(sem):
    return pltpu.CompilerParams(dimension_semantics=sem, vmem_limit_bytes=VMEM_LIMIT)


def _sigmoid(x):
    return 0.5 * jnp.tanh(0.5 * x) + 0.5


def _softplus(x):
    return jnp.maximum(x, 0.0) + jnp.log(1.0 + jnp.exp(-jnp.abs(x)))


def _rms(x, gain):
    return x * lax.rsqrt(jnp.mean(x * x, axis=-1, keepdims=True) + EPS) * gain


def _proj_kernel(x_ref, g_ref, w_ref, ws_ref, wv_ref, o_ref, os_ref, vt_ref, u_ref):
    @pl.when(pl.program_id(1) == 0)
    def _():
        u = _rms(x_ref[...], g_ref[...]).astype(BF16)
        u_ref[...] = u
        os_ref[...] = jnp.dot(u, ws_ref[...], preferred_element_type=F32)
        vt_ref[0] = jnp.dot(u, wv_ref[...], preferred_element_type=F32).T.astype(BF16)

    o_ref[...] = jnp.dot(u_ref[...], w_ref[...], preferred_element_type=F32).astype(BF16)


def _proj_in(x2, gain, w_main, w_small, w_fvt, l, S, tm=1024, tn=1024):
    T = x2.shape[0]
    nS = S // tm
    return pl.pallas_call(
        _proj_kernel,
        grid=(T // tm, N_MAIN // tn),
        in_specs=[
            pl.BlockSpec((tm, D_MODEL), lambda i, j: (i, 0), pipeline_mode=pl.Buffered(1)),
            pl.BlockSpec((1, D_MODEL), lambda i, j: (0, 0)),
            pl.BlockSpec((None, D_MODEL, tn), lambda i, j: (l, 0, j)),
            pl.BlockSpec((None, D_MODEL, N_SMALL), lambda i, j: (l, 0, 0)),
            pl.BlockSpec((None, D_MODEL, BRANCH_W), lambda i, j: (l, 0, 0)),
        ],
        out_specs=[
            pl.BlockSpec((tm, tn), lambda i, j: (i, j)),
            pl.BlockSpec((tm, N_SMALL), lambda i, j: (i, 0)),
            pl.BlockSpec((1, BRANCH_W, tm), lambda i, j: (i // nS, 0, i % nS)),
        ],
        out_shape=[
            jax.ShapeDtypeStruct((T, N_MAIN), BF16),
            jax.ShapeDtypeStruct((T, N_SMALL), F32),
            jax.ShapeDtypeStruct((T // S, BRANCH_W, S), BF16),
        ],
        scratch_shapes=[pltpu.VMEM((tm, D_MODEL), BF16)],
        compiler_params=_cparams(("parallel", "arbitrary")),
        name="proj_in",
    )(x2, gain, w_main, w_small, w_fvt)


def _cumsum_kernel(s_ref, b_ref, al_ref, c_ref, ct_ref, gt_ref):
    S = s_ref.shape[1]
    x = s_ref[0] + b_ref[...]
    lane = lax.broadcasted_iota(jnp.int32, (S, LANES), 1)
    gt_ref[0] = jnp.where(lane < S_GA, _sigmoid(x), -jnp.exp(al_ref[...]) * _softplus(x))
    logf = -_softplus(-x) * LOG2E
    r = lax.broadcasted_iota(jnp.int32, (LANES, LANES), 0)
    c = lax.broadcasted_iota(jnp.int32, (LANES, LANES), 1)
    tri = (r >= c).astype(F32)
    carry = jnp.zeros((1, LANES), F32)
    for blk in range(S // LANES):
        seg = logf[blk * LANES:(blk + 1) * LANES]
        cs = jnp.dot(tri, seg, precision=HIGHEST, preferred_element_type=F32) + carry
        c_ref[0, blk * LANES:(blk + 1) * LANES, :] = cs
        carry = cs[LANES - 1:LANES, :]
    ct_ref[0] = c_ref[0].T[:HEADS]


def _gates(small3, bias_row, alog_row):
    B, S, _ = small3.shape
    return pl.pallas_call(
        _cumsum_kernel,
        grid=(B,),
        in_specs=[
            pl.BlockSpec((1, S, LANES), lambda b: (b, 0, 2)),
            pl.BlockSpec((1, LANES), lambda b: (0, 0)),
            pl.BlockSpec((1, LANES), lambda b: (0, 0)),
        ],
        out_specs=[
            pl.BlockSpec((1, S, LANES), lambda b: (b, 0, 0)),
            pl.BlockSpec((1, HEADS, S), lambda b: (b, 0, 0)),
            pl.BlockSpec((1, S, LANES), lambda b: (b, 0, 0)),
        ],
        out_shape=[
            jax.ShapeDtypeStruct((B, S, LANES), F32),
            jax.ShapeDtypeStruct((B, HEADS, S), F32),
            jax.ShapeDtypeStruct((B, S, LANES), F32),
        ],
        compiler_params=_cparams(("parallel",)),
        name="gates",
    )(small3, bias_row, alog_row)


def _attn_kernel(*refs, dk, scale, decay, tq):
    if decay:
        q_ref, k_ref, vt_ref, cq_ref, ck_ref, o_ref = refs
    else:
        q_ref, k_ref, vt_ref, o_ref = refs
    i = pl.program_id(1)
    krow = lax.broadcasted_iota(jnp.int32, (tq, tq), 0)
    qcol = lax.broadcasted_iota(jnp.int32, (tq, tq), 1)
    causal = krow <= qcol
    c2 = scale * LOG2E

    def group_step(hs, qs, cqs, j, carry, masked):
        off = pl.multiple_of(j * tq, tq)
        ts = []
        for h, q in zip(hs, qs):
            k = k_ref[0, pl.ds(off, tq), h * dk:(h + 1) * dk]
            t = lax.dot_general(k, q, (((1,), (1,)), ((), ())), preferred_element_type=F32) * c2
            if decay:
                t = t - ck_ref[0, pl.ds(off, tq), h:h + 1]
            if masked:
                t = jnp.where(causal, t, -jnp.inf)
            ts.append(t)
        ps = []
        for t, cq, (m, l, acc) in zip(ts, cqs, carry):
            mt = jnp.max(t, axis=0, keepdims=True)
            m_new = jnp.maximum(m, mt + cq if decay else mt)
            p = jnp.exp2(t - (m_new - cq if decay else m_new))
            alpha = jnp.exp2(m - m_new)
            ps.append((m_new, alpha, alpha * l + jnp.sum(p, axis=0, keepdims=True), p.astype(BF16)))
        out = []
        for h, (m_new, alpha, l, p), (_, _, acc) in zip(hs, ps, carry):
            vt = vt_ref[0, h * HEAD_DIM:(h + 1) * HEAD_DIM, pl.ds(off, tq)]
            out.append((m_new, l, alpha * acc + jnp.dot(vt, p, preferred_element_type=F32)))
        return tuple(out)

    for h0 in range(0, HEADS, ATTN_GROUP):
        hs = tuple(range(h0, h0 + ATTN_GROUP))
        qs = [q_ref[0, :, h * dk:(h + 1) * dk] for h in hs]
        cqs = [cq_ref[0, h:h + 1, :] if decay else None for h in hs]

        def step(j, carry, masked, hs=hs, qs=qs, cqs=cqs):
            return group_step(hs, qs, cqs, j, carry, masked)

        init = tuple((jnp.full((1, tq), -jnp.inf, F32), jnp.zeros((1, tq), F32),
                      jnp.zeros((HEAD_DIM, tq), F32)) for _ in hs)
        carry = lax.fori_loop(0, i, functools.partial(step, masked=False), init)
        for h, (_, l, acc) in zip(hs, step(i, carry, masked=True)):
            o_ref[0, :, h * HEAD_DIM:(h + 1) * HEAD_DIM] = (acc / l).T.astype(BF16)


def _attention(q_arr, k_arr, vt_arr, q_blk, k_blk, dk, scale, cum=None, tq=512):
    B, S, _ = q_arr.shape
    decay = cum is not None
    in_specs = [
        pl.BlockSpec((1, tq, HEADS * dk), lambda b, i: (b, i, q_blk)),
        pl.BlockSpec((1, S, HEADS * dk), lambda b, i: (b, 0, k_blk)),
        pl.BlockSpec((1, BRANCH_W, S), lambda b, i: (b, 0, 0)),
    ]
    args = [q_arr, k_arr, vt_arr]
    if decay:
        in_specs += [
            pl.BlockSpec((1, HEADS, tq), lambda b, i: (b, 0, i)),
            pl.BlockSpec((1, S, LANES), lambda b, i: (b, 0, 0)),
        ]
        args += [cum[1], cum[0]]
    return pl.pallas_call(
        functools.partial(_attn_kernel, dk=dk, scale=scale, decay=decay, tq=tq),
        grid=(B, S // tq),
        in_specs=in_specs,
        out_specs=pl.BlockSpec((1, tq, BRANCH_W), lambda b, i: (b, i, 0)),
        out_shape=jax.ShapeDtypeStruct((B, S, BRANCH_W), BF16),
        compiler_params=_cparams(("parallel", "arbitrary")),
        name="fox_attn" if decay else "mla_attn",
    )(*args)


def _mla_prep_kernel(cq_ref, ckv_ref, sm_ref, qn_ref, kvn_ref, wqa_ref, wqb_ref, wk_ref, wv_ref,
                     cos_ref, sin_ref, q_ref, k_ref, v_ref):
    cos = cos_ref[...]
    sin = sin_ref[...]
    qn = _rms(cq_ref[...].astype(F32), qn_ref[...]).astype(BF16)
    a = jnp.dot(qn, wqa_ref[...], preferred_element_type=F32)
    bsw = jnp.dot(qn, wqb_ref[...], preferred_element_type=F32)
    kvn = _rms(ckv_ref[...].astype(F32), kvn_ref[...]).astype(BF16)
    kn = jnp.dot(kvn, wk_ref[...], preferred_element_type=F32)
    v_ref[0] = lax.dot_general(wv_ref[...], kvn, (((1,), (1,)), ((), ())),
                               preferred_element_type=F32).astype(BF16)
    sm = sm_ref[...]
    kpe = (sm[:, 0:LANES] * cos + sm[:, LANES:2 * LANES] * sin).astype(BF16)
    for h in range(HEADS):
        lo = h * MLA_QK
        q_ref[:, lo:lo + LANES] = a[:, lo:lo + LANES].astype(BF16)
        q_ref[:, lo + LANES:lo + 2 * LANES] = (
            a[:, lo + LANES:lo + 2 * LANES] * cos + bsw[:, h * LANES:(h + 1) * LANES] * sin
        ).astype(BF16)
        k_ref[:, lo:lo + LANES] = kn[:, h * LANES:(h + 1) * LANES].astype(BF16)
        k_ref[:, lo + LANES:lo + 2 * LANES] = kpe


def _mla_prep(p_main, p_small, qn, kvn, wqa, wqb, wk, wv, cos_t, sin_t, S, tm=512):
    T = p_main.shape[0]
    nS = S // tm
    const = lambda i: (0, 0)
    return pl.pallas_call(
        _mla_prep_kernel,
        grid=(T // tm,),
        in_specs=[
            pl.BlockSpec((tm, MLA_RANK), lambda i: (i, C_CQ // MLA_RANK)),
            pl.BlockSpec((tm, MLA_RANK), lambda i: (i, C_CKV // MLA_RANK)),
            pl.BlockSpec((tm, N_SMALL), lambda i: (i, 0)),
            pl.BlockSpec((1, MLA_RANK), const),
            pl.BlockSpec((1, MLA_RANK), const),
            pl.BlockSpec((MLA_RANK, HEADS * MLA_QK), const),
            pl.BlockSpec((MLA_RANK, HEADS * LANES), const),
            pl.BlockSpec((MLA_RANK, BRANCH_W), const),
            pl.BlockSpec((BRANCH_W, MLA_RANK), const),
            pl.BlockSpec((tm, LANES), lambda i: (i % nS, 0)),
            pl.BlockSpec((tm, LANES), lambda i: (i % nS, 0)),
        ],
        out_specs=[
            pl.BlockSpec((tm, HEADS * MLA_QK), lambda i: (i, 0)),
            pl.BlockSpec((tm, HEADS * MLA_QK), lambda i: (i, 0)),
            pl.BlockSpec((1, BRANCH_W, tm), lambda i: (i // nS, 0, i % nS)),
        ],
        out_shape=[
            jax.ShapeDtypeStruct((T, HEADS * MLA_QK), BF16),
            jax.ShapeDtypeStruct((T, HEADS * MLA_QK), BF16),
            jax.ShapeDtypeStruct((T // S, BRANCH_W, S), BF16),
        ],
        compiler_params=_cparams(("parallel",)),
        name="mla_prep",
    )(p_main, p_main, p_small, qn, kvn, wqa, wqb, wk, wv, cos_t, sin_t)


def _gdn_kernel(q_ref, k_ref, v_ref, z_ref, gt_ref, cwq_ref, cwk_ref, cwv_ref, on_ref,
                mk_ref, o_ref, qn_s, kn_s, vn_s, g_s, b_s, p_s, n_s, qp_s, oo_s, st_s, cd_s, xp_s):
    h0 = pl.program_id(1) * GDN_HP
    S = q_ref.shape[1]
    C = GDN_C
    NC = S // C
    lane = lax.broadcasted_iota(jnp.int32, (S, LANES), 1)

    xp_s[0:8, :] = jnp.zeros((8, LANES), F32)

    def conv_silu(x_ref, cw_ref, hh):
        cols = pl.ds(pl.multiple_of(hh * LANES, LANES), LANES)
        w = cw_ref[:, cols] * 0.5
        xp_s[8:, :] = x_ref[0, :, cols].astype(F32)
        hy = None
        for sft in range(GDN_CONV):
            t = xp_s[8 - sft:8 - sft + S, :] * w[GDN_CONV - 1 - sft:GDN_CONV - sft]
            hy = t if hy is None else hy + t
        return hy + hy * jnp.tanh(hy)

    def l2n(x):
        return x * lax.rsqrt(jnp.sum(x * x, axis=-1, keepdims=True) + EPS)

    def prologue(hh, _):
        rows = pl.ds(pl.multiple_of(hh * S, S), S)
        qn_s[rows, :] = l2n(conv_silu(q_ref, cwq_ref, hh)) * (HEAD_DIM ** -0.5)
        kn_s[rows, :] = l2n(conv_silu(k_ref, cwk_ref, hh))
        vn_s[rows, :] = conv_silu(v_ref, cwv_ref, hh)
        gates = gt_ref[0]
        pick = lambda col: jnp.broadcast_to(
            jnp.sum(jnp.where(lane == col + h0 + hh, gates, 0.0), axis=-1, keepdims=True),
            (S, LANES))
        b_s[rows, :] = pick(S_GB)
        g_s[rows, :] = pick(S_GA)
        return 0

    lax.fori_loop(0, GDN_HP, prologue, 0)

    tri = mk_ref[0]
    eye = mk_ref[1]
    incl = tri > 0.5
    strict = tri - eye

    nt = (((1,), (1,)), ((), ()))
    dot = functools.partial(jnp.dot, preferred_element_type=F32)

    def local(n, _):
        G = range(GDN_GROUP)
        sls = [pl.ds(pl.multiple_of((n * GDN_GROUP + g) * C, C), C) for g in G]
        q = [qn_s[sl, :] for sl in sls]
        k = [kn_s[sl, :] for sl in sls]
        beta = [b_s[sl, :] for sl in sls]
        gc = [jnp.dot(tri, g_s[sl, :], precision=HIGHEST, preferred_element_type=F32) for sl in sls]
        decay = [jnp.exp(jnp.where(incl, x - x.T, -jnp.inf)) for x in gc]
        kb = [k[g] * beta[g] for g in G]
        kbf = [x.astype(BF16) for x in k]
        a = [lax.dot_general(kb[g].astype(BF16), kbf[g], nt, preferred_element_type=F32)
             * decay[g] * strict for g in G]
        intra = [(lax.dot_general(q[g].astype(BF16), kbf[g], nt, preferred_element_type=F32)
                  * decay[g]).astype(BF16) for g in G]
        t = [eye - a[g] * mk_ref[2] for g in G]
        for lvl in range(1, int(math.log2(C))):
            tb = [x.astype(BF16) for x in t]
            tc = [dot(tb[g], (a[g] * mk_ref[2 + lvl]).astype(BF16)).astype(BF16) for g in G]
            t = [t[g] - dot(tc[g], tb[g]) for g in G]
        tb = [x.astype(BF16) for x in t]
        eg = [jnp.exp(x) for x in gc]
        ub = [dot(tb[g], (vn_s[sls[g], :] * beta[g]).astype(BF16)).astype(BF16) for g in G]
        wb = [dot(tb[g], (kb[g] * eg[g]).astype(BF16)).astype(BF16) for g in G]
        glast = [x[C - 1:C, :] for x in gc]
        kdt = [(k[g] * jnp.exp(glast[g] - gc[g])).T.astype(BF16) for g in G]
        for g in G:
            p_s[sls[g], :] = dot(kdt[g], wb[g]).astype(BF16)
        for g in G:
            n_s[sls[g], :] = dot(kdt[g], ub[g])
        for g in G:
            qp_s[sls[g], :] = (q[g] * eg[g] - dot(intra[g], wb[g])).astype(BF16)
        for g in G:
            oo_s[sls[g], :] = dot(intra[g], ub[g])
            cd_s[n * GDN_GROUP + g] = jnp.broadcast_to(jnp.exp(glast[g]), (8, LANES))
        return 0

    lax.fori_loop(0, GDN_HP * NC // GDN_GROUP, local, 0)

    states = [jnp.zeros((HEAD_DIM, HEAD_DIM), F32) for _ in range(GDN_HP)]
    for n in range(NC):
        rows = [slice((hh * NC + n) * C, (hh * NC + n + 1) * C) for hh in range(GDN_HP)]
        sbs = [s.astype(BF16) for s in states]
        for r, sb in zip(rows, sbs):
            st_s[r, :] = sb
        ps = [dot(p_s[r, :], sb) for r, sb in zip(rows, sbs)]
        states = [states[hh] * cd_s[hh * NC + n][0:1, :] + n_s[rows[hh], :] - ps[hh]
                  for hh in range(GDN_HP)]

    for hh in range(GDN_HP):
        def emit(n, _, hh=hh):
            G = range(GDN_GROUP)
            offs = [pl.multiple_of((n * GDN_GROUP + g) * C, C) for g in G]
            sls = [pl.ds(hh * S + off, C) for off in offs]
            outs = [dot(qp_s[sl, :], st_s[sl, :]) + oo_s[sl, :] for sl in sls]
            for off, out in zip(offs, outs):
                z = z_ref[0, pl.ds(off, C), hh * LANES:(hh + 1) * LANES].astype(F32)
                o_ref[0, pl.ds(off, C), hh * LANES:(hh + 1) * LANES] = (
                    _rms(out, on_ref[...]) * (z * _sigmoid(z))).astype(BF16)
            return 0

        lax.fori_loop(0, NC // GDN_GROUP, emit, 0)


def _gdn_masks():
    C = GDN_C
    i = np.arange(C)[:, None]
    j = np.arange(C)[None, :]
    mats = [(i >= j), (i == j)]
    s = 1
    while s < C:
        mats.append(((i // (2 * s)) == (j // (2 * s))) & ((i // s) % 2 == 1) & ((j // s) % 2 == 0))
        s *= 2
    return jnp.asarray(np.stack(mats).astype(np.float32))


def _gdn(p_main3, gates, conv_w, out_norm, masks):
    B, S, _ = p_main3.shape
    assert S % (GDN_C * GDN_GROUP) == 0, S
    nm = masks.shape[0]
    W = GDN_HP * LANES
    cb = lambda base: (lambda b, h: (b, 0, base // W + h))
    cw = lambda base: (lambda b, h: (0, base // W + h))
    seq = lambda dt=F32: pltpu.VMEM((GDN_HP * S, LANES), dt)
    return pl.pallas_call(
        _gdn_kernel,
        grid=(B, HEADS // GDN_HP),
        in_specs=[
            pl.BlockSpec((1, S, W), cb(C_GQ)),
            pl.BlockSpec((1, S, W), cb(C_GK)),
            pl.BlockSpec((1, S, W), cb(C_GV)),
            pl.BlockSpec((1, S, W), cb(C_GZ)),
            pl.BlockSpec((1, S, LANES), lambda b, h: (b, 0, 0)),
            pl.BlockSpec((GDN_CONV, W), cw(0)),
            pl.BlockSpec((GDN_CONV, W), cw(BRANCH_W)),
            pl.BlockSpec((GDN_CONV, W), cw(2 * BRANCH_W)),
            pl.BlockSpec((1, LANES), lambda b, h: (0, 0)),
            pl.BlockSpec((nm, GDN_C, GDN_C), lambda b, h: (0, 0, 0)),
        ],
        out_specs=pl.BlockSpec((1, S, W), lambda b, h: (b, 0, h)),
        out_shape=jax.ShapeDtypeStruct((B, S, BRANCH_W), BF16),
        scratch_shapes=[seq() for _ in range(5)]
        + [seq(BF16), seq(), seq(BF16), seq(), seq(BF16)]
        + [pltpu.VMEM((GDN_HP * S // GDN_C, 8, LANES), F32), pltpu.VMEM((S + 8, LANES), F32)],
        compiler_params=_cparams(("parallel", "arbitrary")),
        name="gdn",
    )(p_main3, p_main3, p_main3, p_main3, gates, conv_w, conv_w, conv_w, out_norm, masks)


def _merge_kernel(of_ref, om_ref, og_ref, wb_ref, g0_ref, g1_ref, g2_ref, o_ref):
    acc = None
    for o, g, br in ((of_ref, g0_ref, 0), (om_ref, g1_ref, 1), (og_ref, g2_ref, 2)):
        y = _sigmoid(g[...].astype(F32)) * jnp.dot(o[...], wb_ref[br], preferred_element_type=F32)
        acc = y if acc is None else acc + y
    o_ref[...] = acc.astype(BF16)


def _merge(o_fox, o_mla, o_gdn, w_branch, l, p_main, tm=1024, tn=512):
    T = o_fox.shape[0]
    osp = pl.BlockSpec((tm, BRANCH_W), lambda i, j: (i, 0))
    gsp = lambda br: pl.BlockSpec((tm, tn), lambda i, j: (i, (C_GATE + br * D_MODEL) // tn + j))
    return pl.pallas_call(
        _merge_kernel,
        grid=(T // tm, D_MODEL // tn),
        in_specs=[osp, osp, osp,
                  pl.BlockSpec((None, 3, BRANCH_W, tn), lambda i, j: (l, 0, 0, j)),
                  gsp(0), gsp(1), gsp(2)],
        out_specs=pl.BlockSpec((tm, tn), lambda i, j: (i, j)),
        out_shape=jax.ShapeDtypeStruct((T, D_MODEL), BF16),
        compiler_params=_cparams(("parallel", "arbitrary")),
        name="merge",
    )(o_fox, o_mla, o_gdn, w_branch, p_main, p_main, p_main)


def _outproj_kernel(m_ref, w_ref, x_ref, o_ref):
    o_ref[...] = x_ref[...] + jnp.dot(m_ref[...], w_ref[...], preferred_element_type=F32)


def _outproj(merged, w_out, l, x2, tm=1024, tn=1024):
    T = merged.shape[0]
    return pl.pallas_call(
        _outproj_kernel,
        grid=(T // tm, D_MODEL // tn),
        in_specs=[
            pl.BlockSpec((tm, D_MODEL), lambda i, j: (i, 0)),
            pl.BlockSpec((None, D_MODEL, tn), lambda i, j: (l, 0, j)),
            pl.BlockSpec((tm, tn), lambda i, j: (i, j)),
        ],
        out_specs=pl.BlockSpec((tm, tn), lambda i, j: (i, j)),
        out_shape=jax.ShapeDtypeStruct((T, D_MODEL), F32),
        compiler_params=_cparams(("parallel", "arbitrary")),
        name="outproj",
    )(merged, w_out, x2)


def _ffn_kernel(x_ref, g_ref, wu_ref, wd_ref, fg_ref, o_ref, h_ref, *, final):
    f = pl.program_id(1)

    @pl.when(f == 0)
    def _():
        x = x_ref[...]
        h_ref[...] = _rms(x, g_ref[...]).astype(BF16)
        o_ref[...] = x

    hid = jnp.maximum(jnp.dot(h_ref[...], wu_ref[...], preferred_element_type=F32), 0.0)
    o_ref[...] += jnp.dot((hid * hid).astype(BF16), wd_ref[...], preferred_element_type=F32)

    if final:
        @pl.when(f == pl.num_programs(1) - 1)
        def _():
            o_ref[...] = _rms(o_ref[...], fg_ref[...])


def _ffn(x2, gain, w_up, w_down, l, final_gain, final, tm=1024, tf=1024):
    T = x2.shape[0]
    return pl.pallas_call(
        functools.partial(_ffn_kernel, final=final),
        grid=(T // tm, D_FF // tf),
        in_specs=[
            pl.BlockSpec((tm, D_MODEL), lambda i, f: (i, 0), pipeline_mode=pl.Buffered(1)),
            pl.BlockSpec((1, D_MODEL), lambda i, f: (0, 0)),
            pl.BlockSpec((None, D_MODEL, tf), lambda i, f: (l, 0, f)),
            pl.BlockSpec((None, tf, D_MODEL), lambda i, f: (l, f, 0)),
            pl.BlockSpec((1, D_MODEL), lambda i, f: (0, 0)),
        ],
        out_specs=pl.BlockSpec((tm, D_MODEL), lambda i, f: (i, 0)),
        out_shape=jax.ShapeDtypeStruct((T, D_MODEL), F32),
        scratch_shapes=[pltpu.VMEM((tm, D_MODEL), BF16)],
        compiler_params=_cparams(("parallel", "arbitrary")),
        name="ffn",
    )(x2, gain, w_up, w_down, final_gain)


def _rope_tables(S):
    inv_freq = ROPE_THETA ** (-jnp.arange(0, MLA_ROPE, 2, dtype=F32) / MLA_ROPE)
    ang = jnp.arange(S, dtype=F32)[:, None] * inv_freq[None, :]
    cos, sin = jnp.cos(ang), jnp.sin(ang)
    zeros = jnp.zeros((S, LANES - MLA_ROPE), F32)
    return (jnp.concatenate([cos, cos, zeros], axis=1),
            jnp.concatenate([-sin, sin, zeros], axis=1))


def _swap_halves(w):
    half = MLA_ROPE // 2
    return jnp.concatenate([w[..., half:], w[..., :half]], axis=-1)


def _split_w_in(w):
    sizes = (BRANCH_W, BRANCH_W, BRANCH_W, HEADS, MLA_RANK, MLA_RANK, MLA_ROPE,
             3 * BRANCH_W, BRANCH_W, HEADS, HEADS, 3 * D_MODEL)
    offs = np.cumsum((0,) + sizes)
    seg = lambda n: w[..., offs[n]:offs[n + 1]]
    fq, fk, fv, ff, cq, ckv, kpe, gqkv, gz, gb, ga, gate = (seg(n) for n in range(12))
    main = jnp.concatenate([fq, fk, cq, ckv, gqkv, gz, gate], axis=-1).astype(BF16)
    z = lambda n: jnp.zeros(w.shape[:-1] + (n,), F32)
    small = jnp.concatenate(
        [kpe, z(LANES - MLA_ROPE), _swap_halves(kpe), z(LANES - MLA_ROPE),
         ff, gb, ga, z(LANES - 3 * HEADS)], axis=-1).astype(BF16)
    return main, small, fv.astype(BF16)


def _split_mla(w_uq, w_ukv):
    nope, pe = w_uq[..., :HEAD_DIM], w_uq[..., HEAD_DIM:]
    zpad = jnp.zeros((MLA_RANK, HEADS, LANES - MLA_ROPE), F32)
    wqa = jnp.concatenate([nope, pe, zpad], axis=-1).reshape(MLA_RANK, HEADS * MLA_QK)
    wqb = jnp.concatenate([_swap_halves(pe), zpad], axis=-1).reshape(MLA_RANK, HEADS * LANES)
    wk = w_ukv[..., :HEAD_DIM].reshape(MLA_RANK, BRANCH_W)
    wvt = w_ukv[..., HEAD_DIM:].reshape(MLA_RANK, BRANCH_W).T
    return tuple(a.astype(BF16) for a in (wqa, wqb, wk, wvt))


@jax.jit
def _forward(x, attn_norm, w_in, fox_fgate_bias, mla_q_norm, mla_kv_norm, w_mla_uq, w_mla_ukv,
             gdn_conv, gdn_a_log, gdn_dt_bias, gdn_out_norm, w_branch, w_out, mlp_norm,
             w_up, w_down, final_norm):
    B, S, D = x.shape
    T = B * S
    depth = w_in.shape[0]
    cos_t, sin_t = _rope_tables(S)
    masks = _gdn_masks()
    row = lambda v: v.reshape(1, -1).astype(F32)
    x2 = x.reshape(T, D)
    w_main, w_small, w_fvt = _split_w_in(w_in)
    wb_all, wo_all = w_branch.astype(BF16), w_out.astype(BF16)
    wu_all, wd_all = w_up.astype(BF16), w_down.astype(BF16)
    for l in range(depth):
        p_main, p_small, fv_t = _proj_in(x2, row(attn_norm[l]), w_main, w_small, w_fvt, l, S)
        pm3 = p_main.reshape(B, S, N_MAIN)
        ps3 = p_small.reshape(B, S, N_SMALL)

        bias_row = (jnp.zeros((1, LANES), F32).at[0, S_FF:S_FF + HEADS].set(fox_fgate_bias[l])
                    .at[0, S_GA:S_GA + HEADS].set(gdn_dt_bias[l]))
        alog_row = jnp.zeros((1, LANES), F32).at[0, S_GA:S_GA + HEADS].set(gdn_a_log[l])
        cum_c, cum_r, gdn_gates = _gates(ps3, bias_row, alog_row)
        cum = (cum_c, cum_r)
        o_fox = _attention(pm3, pm3, fv_t, C_FQ // BRANCH_W, C_FK // BRANCH_W,
                           HEAD_DIM, HEAD_DIM ** -0.5, cum=cum)

        wqa, wqb, wk, wvt = _split_mla(w_mla_uq[l], w_mla_ukv[l])
        qf, kf, mv_t = _mla_prep(p_main, p_small, row(mla_q_norm[l]), row(mla_kv_norm[l]),
                                 wqa, wqb, wk, wvt, cos_t, sin_t, S)
        shp = lambda a: a.reshape(B, S, a.shape[-1])
        o_mla = _attention(shp(qf), shp(kf), mv_t, 0, 0, MLA_QK, (HEAD_DIM + MLA_ROPE) ** -0.5)

        o_gdn = _gdn(pm3, gdn_gates, gdn_conv[l], row(gdn_out_norm[l]), masks)

        merged = _merge(o_fox.reshape(T, BRANCH_W), o_mla.reshape(T, BRANCH_W),
                        o_gdn.reshape(T, BRANCH_W), wb_all, l, p_main)
        x2 = _outproj(merged, wo_all, l, x2)
        x2 = _ffn(x2, row(mlp_norm[l]), wu_all, wd_all, l, row(final_norm),
                  final=(l == depth - 1))
    return x2.reshape(B, S, D)


def kernel(x, attn_norm, w_in, fox_fgate_bias, mla_q_norm, mla_kv_norm, w_mla_uq, w_mla_ukv,
           gdn_conv, gdn_a_log, gdn_dt_bias, gdn_out_norm, w_branch, w_out, mlp_norm,
           w_up, w_down, final_norm):
    return _forward(x, attn_norm, w_in, fox_fgate_bias, mla_q_norm, mla_kv_norm, w_mla_uq,
                    w_mla_ukv, gdn_conv, gdn_a_log, gdn_dt_bias, gdn_out_norm, w_branch, w_out,
                    mlp_norm, w_up, w_down, final_norm)
```
